```python
import math
import jax, jax.numpy as jnp
from jax import lax
import numpy as np

D_MODEL = 1024
BATCH = 4
SEQ = 8192
DEPTH = 1

EPS = 1e-6
D_RNN = D_MODEL
N_RNN_BLOCKS = 8
RNN_BLOCK = D_RNN // N_RNN_BLOCKS
RNN_CONV = 4
RG_C = 8.0
HEAD_DIM = 64
N_HEADS = D_MODEL // (2 * HEAD_DIM)
V_HEAD_DIM = 2 * HEAD_DIM
D_QK = N_HEADS * 2 * HEAD_DIM
D_V = N_HEADS * V_HEAD_DIM
ATTN_SCALE = HEAD_DIM ** -0.5
Q_BLOCK = 128
D_FF = 3 * D_MODEL
FFN_CONV = 3
SPLITS = (D_RNN, 2 * D_RNN, 2 * D_RNN + D_QK, 2 * D_RNN + 2 * D_QK,
          2 * D_RNN + 2 * D_QK + D_V, 2 * D_RNN + 2 * D_QK + D_V + D_MODEL)
D_IN = SPLITS[-1] + D_MODEL

kernel_name = "hybrid_rglru_diffattn_convffn"


def lambda_init_for(layer_idx):
    return 0.8 - 0.6 * math.exp(-0.3 * layer_idx)


def rms_norm(x, g, eps=EPS):
    xf = x.astype(jnp.float32)
    y = xf * lax.rsqrt(jnp.mean(xf * xf, axis=-1, keepdims=True) + eps)
    return (y * g.astype(jnp.float32)).astype(x.dtype)


def causal_dwconv(x, w, b):
    k_width, c = w.shape
    y = lax.conv_general_dilated(
        x, w[:, None, :].astype(x.dtype), window_strides=(1,),
        padding=[(k_width - 1, 0)], dimension_numbers=("NWC", "WIO", "NWC"),
        feature_group_count=c)
    return y + b


def _linear_recurrence(left, right):
    a_l, b_l = left
    a_r, b_r = right
    return a_l * a_r, a_r * b_l + b_r


def rg_lru(x, wa, ba, wx, bx, lam):
    bsz, s, _ = x.shape
    xb = x.reshape(bsz, s, N_RNN_BLOCKS, RNN_BLOCK)
    r = jax.nn.sigmoid(jnp.einsum('bsni,nij->bsnj', xb, wa).reshape(bsz, s, D_RNN) + ba)
    i = jax.nn.sigmoid(jnp.einsum('bsni,nij->bsnj', xb, wx).reshape(bsz, s, D_RNN) + bx)
    log_a = (-RG_C * r.astype(jnp.float32)) * jax.nn.softplus(-lam.astype(jnp.float32))
    a = jnp.exp(log_a)
    mult = jnp.sqrt(jnp.maximum(-jnp.expm1(2.0 * log_a), 0.0))
    b = mult * (i * x).astype(jnp.float32)
    _, h = lax.associative_scan(_linear_recurrence, (a, b), axis=1)
    return h.astype(x.dtype)


def diff_attention(q, k, v, lam):
    bsz, s = q.shape[:2]
    nb = s // Q_BLOCK
    qb = q.reshape(bsz, nb, Q_BLOCK, N_HEADS, 2, HEAD_DIM).transpose(1, 0, 2, 3, 4, 5)
    starts = jnp.arange(nb, dtype=jnp.int32) * Q_BLOCK
    k_pos = jnp.arange(s, dtype=jnp.int32)
    lam32 = lam.astype(jnp.float32)

    def one_block(args):
        q_blk, start = args
        sc = jnp.einsum('bqhmd,bkhmd->bhmqk', q_blk, k).astype(jnp.float32) * ATTN_SCALE
        q_pos = start + jnp.arange(Q_BLOCK, dtype=jnp.int32)
        causal = k_pos[None, :] <= q_pos[:, None]
        p = jax.nn.softmax(jnp.where(causal, sc, -jnp.inf), axis=-1)
        attn = p[:, :, 0] - lam32 * p[:, :, 1]
        return jnp.einsum('bhqk,bkhe->bqhe', attn.astype(v.dtype), v)

    o = lax.map(one_block, (qb, starts))
    return o.transpose(1, 0, 2, 3, 4).reshape(bsz, s, N_HEADS, V_HEAD_DIM)


def setup_inputs(seed: int = 0) -> dict:
    key = jax.random.key(seed)
    ks = jax.random.split(key, 32)
    f32 = jnp.float32

    def nrm(k, shape, scale):
        return jax.random.normal(k, shape, f32) * scale

    def gain(k, shape):
        return 1.0 + 0.02 * jax.random.normal(k, shape, f32)

    L = DEPTH
    x = jax.random.normal(ks[0], (BATCH, SEQ, D_MODEL), f32)
    attn_norm_g = gain(ks[1], (L, D_MODEL))
    w_in = nrm(ks[2], (L, D_MODEL, D_IN), D_MODEL ** -0.5)
    rnn_conv_w = nrm(ks[3], (L, RNN_CONV, D_RNN), RNN_CONV ** -0.5)
    rnn_conv_b = nrm(ks[4], (L, D_RNN), 0.02)
    rg_wa = nrm(ks[5], (L, N_RNN_BLOCKS, RNN_BLOCK, RNN_BLOCK), RNN_BLOCK ** -0.5)
    rg_ba = nrm(ks[6], (L, D_RNN), 0.02)
    rg_wx = nrm(ks[7], (L, N_RNN_BLOCKS, RNN_BLOCK, RNN_BLOCK), RNN_BLOCK ** -0.5)
    rg_bx = nrm(ks[8], (L, D_RNN), 0.02)
    u = jax.random.uniform(ks[9], (L, D_RNN), f32, 0.9, 0.999)
    s_a = u ** (1.0 / RG_C)
    rg_lambda = jnp.log(s_a) - jnp.log1p(-s_a)
    lam_q1 = nrm(ks[10], (L, HEAD_DIM), 0.1)
    lam_k1 = nrm(ks[11], (L, HEAD_DIM), 0.1)
    lam_q2 = nrm(ks[12], (L, HEAD_DIM), 0.1)
    lam_k2 = nrm(ks[13], (L, HEAD_DIM), 0.1)
    subln_g = gain(ks[14], (L, V_HEAD_DIM))
    w_proj_rnn = nrm(ks[15], (L, D_RNN, D_MODEL), D_RNN ** -0.5)
    w_proj_attn = nrm(ks[16], (L, D_V, D_MODEL), D_V ** -0.5)
    w_out = nrm(ks[17], (L, D_MODEL, D_MODEL), D_MODEL ** -0.5)
    mlp_norm_g = gain(ks[18], (L, D_MODEL))
    w_up = nrm(ks[19], (L, D_MODEL, 2 * D_FF), D_MODEL ** -0.5)
    ffn_conv_w = nrm(ks[20], (L, FFN_CONV, D_FF), FFN_CONV ** -0.5)
    ffn_conv_b = nrm(ks[21], (L, D_FF), 0.02)
    w_down = nrm(ks[22], (L, D_FF, D_MODEL), D_FF ** -0.5)
    final_norm_g = gain(ks[23], (D_MODEL,))
    return {"x": x, "attn_norm_g": attn_norm_g, "w_in": w_in,
            "rnn_conv_w": rnn_conv_w, "rnn_conv_b": rnn_conv_b,
            "rg_wa": rg_wa, "rg_ba": rg_ba, "rg_wx": rg_wx, "rg_bx": rg_bx,
            "rg_lambda": rg_lambda, "lam_q1": lam_q1, "lam_k1": lam_k1,
            "lam_q2": lam_q2, "lam_k2": lam_k2, "subln_g": subln_g,
            "w_proj_rnn": w_proj_rnn, "w_proj_attn": w_proj_attn, "w_out": w_out,
            "mlp_norm_g": mlp_norm_g, "w_up": w_up, "ffn_conv_w": ffn_conv_w,
            "ffn_conv_b": ffn_conv_b, "w_down": w_down, "final_norm_g": final_norm_g}


def reference(x, attn_norm_g, w_in, rnn_conv_w, rnn_conv_b, rg_wa, rg_ba, rg_wx, rg_bx,
              rg_lambda, lam_q1, lam_k1, lam_q2, lam_k2, subln_g, w_proj_rnn, w_proj_attn,
              w_out, mlp_norm_g, w_up, ffn_conv_w, ffn_conv_b, w_down, final_norm_g):
    bsz, s, _ = x.shape
    for l in range(DEPTH):
        lambda_init = lambda_init_for(l)
        h = rms_norm(x, attn_norm_g[l])
        proj = h @ w_in[l]
        xr, gr, q, k, v, g_rnn, g_attn = jnp.split(proj, SPLITS, axis=-1)
        xr = causal_dwconv(xr, rnn_conv_w[l], rnn_conv_b[l])
        y_rnn = rg_lru(xr, rg_wa[l], rg_ba[l], rg_wx[l], rg_bx[l], rg_lambda[l])
        y_rnn = y_rnn * jax.nn.gelu(gr)
        lam = (jnp.exp(jnp.sum(lam_q1[l].astype(jnp.float32) * lam_k1[l].astype(jnp.float32)))
               - jnp.exp(jnp.sum(lam_q2[l].astype(jnp.float32) * lam_k2[l].astype(jnp.float32)))
               + lambda_init)
        qh = q.reshape(bsz, s, N_HEADS, 2, HEAD_DIM)
        kh = k.reshape(bsz, s, N_HEADS, 2, HEAD_DIM)
        vh = v.reshape(bsz, s, N_HEADS, V_HEAD_DIM)
        o = diff_attention(qh, kh, vh, lam)
        o = rms_norm(o, subln_g[l], eps=1e-5) * (1.0 - lambda_init)
        y_attn = o.reshape(bsz, s, D_V)
        merged = (jax.nn.sigmoid(g_rnn) * (y_rnn @ w_proj_rnn[l])
                  + jax.nn.sigmoid(g_attn) * (y_attn @ w_proj_attn[l]))
        x = x + merged @ w_out[l]
        h = rms_norm(x, mlp_norm_g[l])
        u_gate, u_val = jnp.split(h @ w_up[l], 2, axis=-1)
        u_gate = causal_dwconv(u_gate, ffn_conv_w[l], ffn_conv_b[l])
        x = x + (jax.nn.gelu(u_gate) * u_val) @ w_down[l]
    return rms_norm(x, final_norm_g)
```

```python
import functools
import math

import jax
import jax.numpy as jnp
from jax import lax
from jax.experimental import pallas as pl
from jax.experimental.pallas import tpu as pltpu

F32 = jnp.float32
BF16 = jnp.bfloat16

D_MODEL = 1024
EPS = 1e-6
SUBLN_EPS = 1e-5
D_RNN = D_MODEL
N_RNN_BLOCKS = 8
RNN_BLOCK = D_RNN // N_RNN_BLOCKS
RNN_CONV = 4
RG_C = 8.0
HEAD_DIM = 64
N_HEADS = D_MODEL // (2 * HEAD_DIM)
V_HEAD_DIM = 2 * HEAD_DIM
ATTN_SCALE = HEAD_DIM ** -0.5
D_FF = 3 * D_MODEL
FFN_CONV = 3
D_IN = 7 * D_MODEL
LAMBDA_INIT = 0.8 - 0.6 * math.exp(-0.3 * 0)

VMEM_LIMIT_BYTES = 56 * 1024 * 1024
HALO = 16

INPROJ_TM = 1024
INPROJ_TN = 1024
RNN_TT = 512
ATTN_T = 512
MERGE_TM = 512
FFN_TM = 1024
FFN_TF = 512


def _rms(x, g, eps):
    return x * lax.rsqrt(jnp.mean(x * x, axis=-1, keepdims=True) + eps) * g


def _inproj_kernel(x_ref, g_ref, w_ref, o_ref, h_ref):
    @pl.when(pl.program_id(1) == 0)
    def _():
        h_ref[...] = _rms(x_ref[...], g_ref[...], EPS).astype(BF16)

    o_ref[...] = jnp.dot(h_ref[...], w_ref[...],
                         preferred_element_type=F32).astype(o_ref.dtype)


def _inproj(x2, g, w):
    t = x2.shape[0]
    tm, tn = INPROJ_TM, INPROJ_TN
    return pl.pallas_call(
        _inproj_kernel,
        grid=(t // tm, D_IN // tn),
        in_specs=[
            pl.BlockSpec((tm, D_MODEL), lambda i, j: (i, 0)),
            pl.BlockSpec((1, D_MODEL), lambda i, j: (0, 0)),
            pl.BlockSpec((D_MODEL, tn), lambda i, j: (0, j)),
        ],
        out_specs=pl.BlockSpec((tm, tn), lambda i, j: (i, j)),
        out_shape=jax.ShapeDtypeStruct((t, D_IN), BF16),
        scratch_shapes=[pltpu.VMEM((tm, D_MODEL), BF16)],
        compiler_params=pltpu.CompilerParams(
            dimension_semantics=("arbitrary", "arbitrary"),
            vmem_limit_bytes=VMEM_LIMIT_BYTES),
        name="in_proj",
    )(x2, g, w)


def _rglru_kernel(xr_ref, gr_ref, halo_ref, cw_ref, cb_ref, wcat_ref, ba_ref, bx_ref,
                  lam_ref, o_ref, carry_ref, a_ref, b_ref, h_ref):
    i = pl.program_id(1)
    tt = xr_ref.shape[0]
    c = D_RNN

    @pl.when(i == 0)
    def _():
        carry_ref[...] = jnp.zeros_like(carry_ref)

    xr = xr_ref[...].astype(F32)
    halo = halo_ref[...].astype(F32)
    halo = jnp.where(i == 0, jnp.zeros_like(halo), halo)
    ext = jnp.concatenate([halo, xr], axis=0)
    cw = cw_ref[...]
    xc = cw[RNN_CONV - 1:RNN_CONV, :] * xr + cb_ref[...]
    for k in range(1, RNN_CONV):
        shifted = pltpu.roll(ext, k, axis=0)[HALO:, :]
        xc = xc + cw[RNN_CONV - 1 - k:RNN_CONV - k, :] * shifted

    sp = jnp.logaddexp(-lam_ref[...], 0.0)
    xcb = xc.astype(BF16)
    for n in range(N_RNN_BLOCKS):
        sl = slice(n * RNN_BLOCK, (n + 1) * RNN_BLOCK)
        g = jnp.dot(xcb[:, sl], wcat_ref[n], preferred_element_type=F32)
        r = jax.nn.sigmoid(g[:, :RNN_BLOCK] + ba_ref[:, sl])
        ig = jax.nn.sigmoid(g[:, RNN_BLOCK:] + bx_ref[:, sl])
        log_a = (-RG_C * r) * sp[:, sl]
        a = jnp.exp(log_a)
        mult = jnp.sqrt(jnp.maximum(1.0 - a * a, 0.0))
        a_ref[:, sl] = a
        b_ref[:, sl] = mult * (ig * xc[:, sl])

    rows = lax.broadcasted_iota(jnp.int32, (8, c), 0)

    def body(g, carry):
        off = pl.multiple_of(g * 8, 8)
        av = a_ref[pl.ds(off, 8), :]
        bv = b_ref[pl.ds(off, 8), :]
        for d in (1, 2, 4):
            keep = rows >= d
            a_sh = jnp.where(keep, pltpu.roll(av, d, axis=0), 1.0)
            b_sh = jnp.where(keep, pltpu.roll(bv, d, axis=0), 0.0)
            bv = av * b_sh + bv
            av = av * a_sh
        h = av * carry + bv
        h_ref[pl.ds(off, 8), :] = h
        return jnp.broadcast_to(h[7:8, :], (8, c))

    carry_ref[...] = lax.fori_loop(0, tt // 8, body, carry_ref[...], unroll=2)
    o_ref[...] = (h_ref[...] * jax.nn.gelu(gr_ref[...].astype(F32))).astype(o_ref.dtype)


def _rglru(proj3, cw, cb, wcat, ba, bx, lam):
    b, s, _ = proj3.shape
    tt = RNN_TT
    hb = tt // HALO
    vec = pl.BlockSpec((1, D_RNN), lambda bi, i: (0, 0))
    return pl.pallas_call(
        _rglru_kernel,
        grid=(b, s // tt),
        in_specs=[
            pl.BlockSpec((None, tt, D_RNN), lambda bi, i: (bi, i, 0)),
            pl.BlockSpec((None, tt, D_RNN), lambda bi, i: (bi, i, 1)),
            pl.BlockSpec((None, HALO, D_RNN), lambda bi, i: (bi, jnp.maximum(i * hb - 1, 0), 0)),
            pl.BlockSpec((RNN_CONV, D_RNN), lambda bi, i: (0, 0)),
            vec,
            pl.BlockSpec((N_RNN_BLOCKS, RNN_BLOCK, 2 * RNN_BLOCK), lambda bi, i: (0, 0, 0)),
            vec, vec, vec,
        ],
        out_specs=pl.BlockSpec((None, tt, D_RNN), lambda bi, i: (bi, i, 0)),
        out_shape=jax.ShapeDtypeStruct((b, s, D_RNN), BF16),
        scratch_shapes=[pltpu.VMEM((8, D_RNN), F32),
                        pltpu.VMEM((tt, D_RNN), F32),
                        pltpu.VMEM((tt, D_RNN), F32),
                        pltpu.VMEM((tt, D_RNN), F32)],
        compiler_params=pltpu.CompilerParams(
            dimension_semantics=("arbitrary", "arbitrary"),
            vmem_limit_bytes=VMEM_LIMIT_BYTES),
        name="rglru",
    )(proj3, proj3, proj3, cw, cb, wcat, ba, bx, lam)


def _attn_kernel(q_ref, k_ref, v_ref, g_ref, lq1_ref, lk1_ref, lq2_ref, lk2_ref,
                 o_ref, vt_ref, qq_ref, m_ref, l_ref, acc_ref):
    qi = pl.program_id(2)
    t = ATTN_T
    nk = vt_ref.shape[0]

    @pl.when(qi == 0)
    def _():
        for j in range(nk):
            vt_ref[j] = v_ref[j * t:(j + 1) * t, :].T

    qs = q_ref[...] * jnp.asarray(ATTN_SCALE, BF16)
    lane = lax.broadcasted_iota(jnp.int32, qs.shape, 1)
    zero = jnp.zeros_like(qs)
    qq_ref[0:t, :] = jnp.where(lane < HEAD_DIM, qs, zero)
    qq_ref[t:, :] = jnp.where(lane >= HEAD_DIM, qs, zero)
    m_ref[...] = jnp.full_like(m_ref, -jnp.inf)
    l_ref[...] = jnp.zeros_like(l_ref)
    acc_ref[...] = jnp.zeros_like(acc_ref)

    def step(ki, masked):
        off = pl.multiple_of(ki * t, t)
        kb = k_ref[pl.ds(off, t), :]
        st = lax.dot_general(kb, qq_ref[...], (((1,), (1,)), ((), ())),
                             preferred_element_type=F32)
        if masked:
            r = lax.broadcasted_iota(jnp.int32, st.shape, 0)
            cidx = lax.broadcasted_iota(jnp.int32, st.shape, 1)
            cidx = jnp.where(cidx >= t, cidx - t, cidx)
            st = jnp.where(r <= cidx, st, -jnp.inf)
        m_old = m_ref[...]
        m_new = jnp.maximum(m_old, jnp.max(st, axis=0, keepdims=True))
        alpha = jnp.exp(m_old - m_new)
        p = jnp.exp(st - m_new)
        l_ref[...] = alpha * l_ref[...] + jnp.sum(p, axis=0, keepdims=True)
        acc_ref[...] = alpha * acc_ref[...] + jnp.dot(
            vt_ref[ki], p.astype(BF16), preferred_element_type=F32)
        m_ref[...] = m_new

    def loop_body(ki, carry):
        step(ki, False)
        return carry

    lax.fori_loop(0, qi, loop_body, 0)
    step(qi, True)

    lam = (jnp.exp(jnp.sum(lq1_ref[...] * lk1_ref[...], axis=-1, keepdims=True))
           - jnp.exp(jnp.sum(lq2_ref[...] * lk2_ref[...], axis=-1, keepdims=True))
           + LAMBDA_INIT)
    o = acc_ref[...] * (1.0 / l_ref[...])
    head = o[:, :t] - lam * o[:, t:]
    ms = jnp.mean(head * head, axis=0, keepdims=True)
    y = head * lax.rsqrt(ms + SUBLN_EPS) * g_ref[...] * (1.0 - LAMBDA_INIT)
    o_ref[...] = y.T.astype(o_ref.dtype)


def _attention(proj3, subln_g, lq1, lk1, lq2, lk2):
    b, s, _ = proj3.shape
    t = ATTN_T
    nq = s // t
    q_blk0 = 2 * D_RNN // V_HEAD_DIM
    k_blk0 = q_blk0 + N_HEADS
    v_blk0 = k_blk0 + N_HEADS
    lvec = pl.BlockSpec((1, HEAD_DIM), lambda bi, h, qi: (0, 0))
    return pl.pallas_call(
        _attn_kernel,
        grid=(b, N_HEADS, nq),
        in_specs=[
            pl.BlockSpec((None, t, V_HEAD_DIM), lambda bi, h, qi: (bi, qi, q_blk0 + h)),
            pl.BlockSpec((None, s, V_HEAD_DIM), lambda bi, h, qi: (bi, 0, k_blk0 + h)),
            pl.BlockSpec((None, s, V_HEAD_DIM), lambda bi, h, qi: (bi, 0, v_blk0 + h)),
            pl.BlockSpec((V_HEAD_DIM, 1), lambda bi, h, qi: (0, 0)),
            lvec, lvec, lvec, lvec,
        ],
        out_specs=pl.BlockSpec((None, t, V_HEAD_DIM), lambda bi, h, qi: (bi, qi, h)),
        out_shape=jax.ShapeDtypeStruct((b, s, D_MODEL), BF16),
        scratch_shapes=[pltpu.VMEM((nq, V_HEAD_DIM, t), BF16),
                        pltpu.VMEM((2 * t, V_HEAD_DIM), BF16),
                        pltpu.VMEM((1, 2 * t), F32),
                        pltpu.VMEM((1, 2 * t), F32),
                        pltpu.VMEM((V_HEAD_DIM, 2 * t), F32)],
        compiler_params=pltpu.CompilerParams(
            dimension_semantics=("arbitrary", "arbitrary", "arbitrary"),
            vmem_limit_bytes=VMEM_LIMIT_BYTES),
        name="diff_attn",
    )(proj3, proj3, proj3, subln_g, lq1, lk1, lq2, lk2)


def _merge_kernel(x_ref, yr_ref, ya_ref, gr_ref, ga_ref, pa_ref, pb_ref, wo_ref, o_ref):
    a = jnp.dot(yr_ref[...], pa_ref[...], preferred_element_type=F32)
    b = jnp.dot(ya_ref[...], pb_ref[...], preferred_element_type=F32)
    merged = (jax.nn.sigmoid(gr_ref[...].astype(F32)) * a
              + jax.nn.sigmoid(ga_ref[...].astype(F32)) * b)
    o_ref[...] = x_ref[...] + jnp.dot(merged.astype(BF16), wo_ref[...],
                                      preferred_element_type=F32)


def _merge(x2, y_rnn, y_attn, proj, pa, pb, wo):
    t = x2.shape[0]
    tm = MERGE_TM
    row = lambda cb: pl.BlockSpec((tm, D_MODEL), lambda i: (i, cb))
    wspec = pl.BlockSpec((D_MODEL, D_MODEL), lambda i: (0, 0))
    return pl.pallas_call(
        _merge_kernel,
        grid=(t // tm,),
        in_specs=[row(0), row(0), row(0), row(5), row(6), wspec, wspec, wspec],
        out_specs=row(0),
        out_shape=jax.ShapeDtypeStruct((t, D_MODEL), F32),
        compiler_params=pltpu.CompilerParams(
            dimension_semantics=("arbitrary",),
            vmem_limit_bytes=VMEM_LIMIT_BYTES),
        name="merge_out_proj",
    )(x2, y_rnn, y_attn, proj, proj, pa, pb, wo)


def _ffn_kernel(tiles_per_seq, x_ref, halo_ref, g_ref, wg_ref, wv_ref, cw_ref, cb_ref,
                wd_ref, fg_ref, o_ref, h_ref, acc_ref):
    i = pl.program_id(0)
    j = pl.program_id(1)
    nj = pl.num_programs(1)

    @pl.when(j == 0)
    def _():
        hh = _rms(halo_ref[...], g_ref[...], EPS)
        hh = jnp.where(i % tiles_per_seq == 0, jnp.zeros_like(hh), hh)
        h_ref[0:HALO, :] = hh.astype(BF16)
        h_ref[HALO:, :] = _rms(x_ref[...], g_ref[...], EPS).astype(BF16)
        acc_ref[...] = jnp.zeros_like(acc_ref)

    hx = h_ref[...]
    u = jnp.dot(hx, wg_ref[...], preferred_element_type=F32)
    uv = jnp.dot(hx[HALO:, :], wv_ref[...], preferred_element_type=F32)
    cw = cw_ref[...]
    y = cw[FFN_CONV - 1:FFN_CONV, :] * u[HALO:, :] + cb_ref[...]
    for k in range(1, FFN_CONV):
        y = y + cw[FFN_CONV - 1 - k:FFN_CONV - k, :] * pltpu.roll(u, k, axis=0)[HALO:, :]
    act = (jax.nn.gelu(y) * uv).astype(BF16)
    acc_ref[...] += jnp.dot(act, wd_ref[...], preferred_element_type=F32)

    @pl.when(j == nj - 1)
    def _():
        o_ref[...] = _rms(x_ref[...] + acc_ref[...], fg_ref[...], EPS)


def _ffn(x1, g, wup, cw, cb, wd, fg, seq_len):
    t = x1.shape[0]
    tm, tf = FFN_TM, FFN_TF
    nf = D_FF // tf
    hb = tm // HALO
    vec = pl.BlockSpec((1, D_MODEL), lambda i, j: (0, 0))
    return pl.pallas_call(
        functools.partial(_ffn_kernel, seq_len // tm),
        grid=(t // tm, nf),
        in_specs=[
            pl.BlockSpec((tm, D_MODEL), lambda i, j: (i, 0)),
            pl.BlockSpec((HALO, D_MODEL), lambda i, j: (jnp.maximum(i * hb - 1, 0), 0)),
            vec,
            pl.BlockSpec((D_MODEL, tf), lambda i, j: (0, j)),
            pl.BlockSpec((D_MODEL, tf), lambda i, j: (0, nf + j)),
            pl.BlockSpec((FFN_CONV, tf), lambda i, j: (0, j)),
            pl.BlockSpec((1, tf), lambda i, j: (0, j)),
            pl.BlockSpec((tf, D_MODEL), lambda i, j: (j, 0)),
            vec,
        ],
        out_specs=pl.BlockSpec((tm, D_MODEL), lambda i, j: (i, 0)),
        out_shape=jax.ShapeDtypeStruct((t, D_MODEL), F32),
        scratch_shapes=[pltpu.VMEM((HALO + tm, D_MODEL), BF16),
                        pltpu.VMEM((tm, D_MODEL), F32)],
        compiler_params=pltpu.CompilerParams(
            dimension_semantics=("arbitrary", "arbitrary"),
            vmem_limit_bytes=VMEM_LIMIT_BYTES),
        name="conv_ffn",
    )(x1, x1, g, wup, wup, cw, cb, wd, fg)


def kernel(x, attn_norm_g, w_in, rnn_conv_w, rnn_conv_b, rg_wa, rg_ba, rg_wx, rg_bx, rg_lambda, lam_q1, lam_k1, lam_q2, lam_k2, subln_g, w_proj_rnn, w_proj_attn, w_out, mlp_norm_g, w_up, ffn_conv_w, ffn_conv_b, w_down, final_norm_g):
    bsz, s, d = x.shape
    t = bsz * s
    x2 = x.reshape(t, d)
    l = 0
    row = lambda v: v.reshape(1, -1)

    proj = _inproj(x2, row(attn_norm_g[l]), w_in[l].astype(BF16))
    proj3 = proj.reshape(bsz, s, D_IN)

    wcat = jnp.concatenate([rg_wa[l], rg_wx[l]], axis=-1).astype(BF16)
    y_rnn = _rglru(proj3, rnn_conv_w[l], row(rnn_conv_b[l]), wcat, row(rg_ba[l]),
                   row(rg_bx[l]), row(rg_lambda[l]))
    y_attn = _attention(proj3, subln_g[l].reshape(V_HEAD_DIM, 1), row(lam_q1[l]),
                        row(lam_k1[l]), row(lam_q2[l]), row(lam_k2[l]))

    x1 = _merge(x2, y_rnn.reshape(t, d), y_attn.reshape(t, d), proj,
                w_proj_rnn[l].astype(BF16), w_proj_attn[l].astype(BF16),
                w_out[l].astype(BF16))
    out = _ffn(x1, row(mlp_norm_g[l]), w_up[l].astype(BF16), ffn_conv_w[l],
               row(ffn_conv_b[l]), w_down[l].astype(BF16), row(final_norm_g), s)
    return out.reshape(bsz, s, d)
```

```python
import functools
import math

import jax
import jax.numpy as jnp
from jax import lax
from jax.experimental import pallas as pl
from jax.experimental.pallas import tpu as pltpu

F32 = jnp.float32
BF16 = jnp.bfloat16

D_MODEL = 1024
EPS = 1e-6
F32_TINY = float(jnp.finfo(jnp.float32).tiny)
SUBLN_EPS = 1e-5
D_RNN = D_MODEL
N_RNN_BLOCKS = 8
RNN_BLOCK = D_RNN // N_RNN_BLOCKS
RNN_CONV = 4
RG_C = 8.0
HEAD_DIM = 64
N_HEADS = D_MODEL // (2 * HEAD_DIM)
V_HEAD_DIM = 2 * HEAD_DIM
ATTN_SCALE = HEAD_DIM ** -0.5
D_FF = 3 * D_MODEL
FFN_CONV = 3
D_IN = 7 * D_MODEL
LAMBDA_INIT = 0.8 - 0.6 * math.exp(-0.3 * 0)

VMEM_LIMIT_BYTES = 56 * 1024 * 1024
HALO = 16

INPROJ_TM = 2048
INPROJ_TN = 1024
RNN_TT = 512
RNN_SCAN_GROUP = 8
ATTN_T = 512
ATTN_CW = 256
ATTN_VT_ROWS = V_HEAD_DIM + 16
ATTN_GROUP = 4
LOG2_E = math.log2(math.e)
MERGE_TM = 512
FFN_TM = 512
FFN_TF = 512


def _rms(x, g, eps):
    return x * lax.rsqrt(jnp.mean(x * x, axis=-1, keepdims=True) + eps) * g


def _inproj_kernel(x_ref, g_ref, w_ref, o_ref, h_ref):
    @pl.when(pl.program_id(1) == 0)
    def _():
        h_ref[...] = _rms(x_ref[...], g_ref[...], EPS).astype(BF16)

    o_ref[...] = jnp.dot(h_ref[...], w_ref[...],
                         preferred_element_type=F32).astype(o_ref.dtype)


def _inproj(x2, g, w):
    t = x2.shape[0]
    tm, tn = INPROJ_TM, INPROJ_TN
    return pl.pallas_call(
        _inproj_kernel,
        grid=(t // tm, D_IN // tn),
        in_specs=[
            pl.BlockSpec((tm, D_MODEL), lambda i, j: (i, 0)),
            pl.BlockSpec((1, D_MODEL), lambda i, j: (0, 0)),
            pl.BlockSpec((D_MODEL, tn), lambda i, j: (0, j)),
        ],
        out_specs=pl.BlockSpec((tm, tn), lambda i, j: (i, j)),
        out_shape=jax.ShapeDtypeStruct((t, D_IN), BF16),
        scratch_shapes=[pltpu.VMEM((tm, D_MODEL), BF16)],
        compiler_params=pltpu.CompilerParams(
            dimension_semantics=("arbitrary", "arbitrary"),
            vmem_limit_bytes=VMEM_LIMIT_BYTES),
        name="in_proj",
    )(x2, g, w)


def _rglru_kernel(xr_ref, gr_ref, halo_ref, cw_ref, cb_ref, wcat_ref, ba_ref, bx_ref,
                  lam_ref, o_ref, carry_ref, ext_ref, a_ref, b_ref, h_ref):
    i = pl.program_id(1)
    tt = xr_ref.shape[0]
    c = D_RNN

    @pl.when(i == 0)
    def _():
        carry_ref[...] = jnp.zeros_like(carry_ref)

    halo = halo_ref[...].astype(F32)
    halo = jnp.where(i == 0, jnp.zeros_like(halo), halo)
    xr = xr_ref[...].astype(F32)
    for n in range(N_RNN_BLOCKS):
        sl = slice(n * RNN_BLOCK, (n + 1) * RNN_BLOCK)
        ext_ref[n, 0:HALO, :] = halo[:, sl]
        ext_ref[n, HALO:, :] = xr[:, sl]

    c4 = (-0.5 * RG_C) * jnp.logaddexp(-lam_ref[...], 0.0)
    for n in range(N_RNN_BLOCKS):
        sl = slice(n * RNN_BLOCK, (n + 1) * RNN_BLOCK)
        xc = cb_ref[:, sl]
        for k in range(RNN_CONV):
            xc = xc + cw_ref[RNN_CONV - 1 - k:RNN_CONV - k, sl] * ext_ref[n, HALO - k:HALO - k + tt, :]
        g = jnp.dot(xc.astype(BF16), wcat_ref[n], preferred_element_type=F32)
        tr = jnp.tanh(g[:, :RNN_BLOCK] + ba_ref[:, sl])
        ti = jnp.tanh(g[:, RNN_BLOCK:] + bx_ref[:, sl])
        log_a = c4[:, sl] * (tr + 1.0)
        a = jnp.exp(log_a)
        y = jnp.maximum(1.0 - a * a, 0.0)
        mult = y * lax.rsqrt(jnp.maximum(y, F32_TINY))
        hx = 0.5 * xc
        a_ref[n] = a
        b_ref[n] = mult * (hx * ti + hx)

    grp = RNN_SCAN_GROUP
    blk = 8 * grp
    rows = lax.broadcasted_iota(jnp.int32, (8, RNN_BLOCK), 0)

    def scan_block(j, carries):
        base = pl.multiple_of(j * blk, blk)
        out = []
        for n in range(N_RNN_BLOCKS):
            cum_a, cum_b = [], []
            for r in range(grp):
                av = a_ref[n, pl.ds(base + r, 8, stride=grp), :]
                bv = b_ref[n, pl.ds(base + r, 8, stride=grp), :]
                if r > 0:
                    bv = av * cum_b[-1] + bv
                    av = av * cum_a[-1]
                cum_a.append(av)
                cum_b.append(bv)
            ga, gb = cum_a[-1], cum_b[-1]
            for d in (1, 2, 4):
                keep = rows >= d
                a_sh = jnp.where(keep, pltpu.roll(ga, d, axis=0), 1.0)
                b_sh = jnp.where(keep, pltpu.roll(gb, d, axis=0), 0.0)
                gb = ga * b_sh + gb
                ga = ga * a_sh
            h_end = ga * carries[n] + gb
            h_in = jnp.where(rows == 0, carries[n], pltpu.roll(h_end, 1, axis=0))
            for r in range(grp):
                h_ref[n, pl.ds(base + r, 8, stride=grp), :] = cum_a[r] * h_in + cum_b[r]
            out.append(jnp.broadcast_to(h_end[7:8, :], (8, RNN_BLOCK)))
        return tuple(out)

    carries = tuple(carry_ref[n] for n in range(N_RNN_BLOCKS))
    carries = lax.fori_loop(0, tt // blk, scan_block, carries, unroll=2)
    for n in range(N_RNN_BLOCKS):
        sl = slice(n * RNN_BLOCK, (n + 1) * RNN_BLOCK)
        carry_ref[n] = carries[n]
        o_ref[:, sl] = (h_ref[n] * jax.nn.gelu(gr_ref[:, sl].astype(F32))).astype(o_ref.dtype)


def _rglru(proj3, cw, cb, wcat, ba, bx, lam):
    b, s, _ = proj3.shape
    tt = RNN_TT
    hb = tt // HALO
    vec = pl.BlockSpec((1, D_RNN), lambda bi, i: (0, 0))
    return pl.pallas_call(
        _rglru_kernel,
        grid=(b, s // tt),
        in_specs=[
            pl.BlockSpec((None, tt, D_RNN), lambda bi, i: (bi, i, 0)),
            pl.BlockSpec((None, tt, D_RNN), lambda bi, i: (bi, i, 1)),
            pl.BlockSpec((None, HALO, D_RNN), lambda bi, i: (bi, jnp.maximum(i * hb - 1, 0), 0)),
            pl.BlockSpec((RNN_CONV, D_RNN), lambda bi, i: (0, 0)),
            vec,
            pl.BlockSpec((N_RNN_BLOCKS, RNN_BLOCK, 2 * RNN_BLOCK), lambda bi, i: (0, 0, 0)),
            vec, vec, vec,
        ],
        out_specs=pl.BlockSpec((None, tt, D_RNN), lambda bi, i: (bi, i, 0)),
        out_shape=jax.ShapeDtypeStruct((b, s, D_RNN), BF16),
        scratch_shapes=[pltpu.VMEM((N_RNN_BLOCKS, 8, RNN_BLOCK), F32),
                        pltpu.VMEM((N_RNN_BLOCKS, HALO + tt, RNN_BLOCK), F32),
                        pltpu.VMEM((N_RNN_BLOCKS, tt, RNN_BLOCK), F32),
                        pltpu.VMEM((N_RNN_BLOCKS, tt, RNN_BLOCK), F32),
                        pltpu.VMEM((N_RNN_BLOCKS, tt, RNN_BLOCK), F32)],
        compiler_params=pltpu.CompilerParams(
            dimension_semantics=("arbitrary", "arbitrary"),
            vmem_limit_bytes=VMEM_LIMIT_BYTES),
        name="rglru",
    )(proj3, proj3, proj3, cw, cb, wcat, ba, bx, lam)


def _attn_kernel(q_ref, k_ref, v_ref, g_ref, lq1_ref, lk1_ref, lq2_ref, lk2_ref,
                 o_ref, vt_ref, qq_ref, m_ref, acc_ref, st_ref):
    qi = pl.program_id(2)
    t = ATTN_T
    cw = ATTN_CW
    nk = vt_ref.shape[0]

    @pl.when(qi == 0)
    def _():
        ones = jnp.ones((ATTN_VT_ROWS - V_HEAD_DIM, t), BF16)
        for j in range(nk):
            vt_ref[j, 0:V_HEAD_DIM, :] = v_ref[j * t:(j + 1) * t, :].T
            vt_ref[j, V_HEAD_DIM:, :] = ones

    qs = (q_ref[...].astype(F32) * (ATTN_SCALE * LOG2_E)).astype(BF16)
    lane = lax.broadcasted_iota(jnp.int32, qs.shape, 1)
    zero = jnp.zeros_like(qs)
    qq_ref[0:t, :] = jnp.where(lane < HEAD_DIM, qs, zero)
    qq_ref[t:, :] = jnp.where(lane >= HEAD_DIM, qs, zero)
    m_ref[...] = jnp.full_like(m_ref, -jnp.inf)
    acc_ref[...] = jnp.zeros_like(acc_ref)

    nch = 2 * t // cw

    def scores(ki, c, slot):
        off = pl.multiple_of(ki * t, t)
        st_ref[slot, c] = lax.dot_general(
            k_ref[pl.ds(off, t), :], qq_ref[c * cw:(c + 1) * cw, :],
            (((1,), (1,)), ((), ())), preferred_element_type=F32)

    def softmax_pv(ki, c, slot, masked):
        cs = slice(c * cw, (c + 1) * cw)
        st = st_ref[slot, c]
        if masked:
            r = lax.broadcasted_iota(jnp.int32, st.shape, 0)
            qidx = lax.broadcasted_iota(jnp.int32, st.shape, 1) + (c * cw) % t
            st = jnp.where(r <= qidx, st, -jnp.inf)
        m_old = m_ref[:, cs]
        m_new = jnp.maximum(m_old, jnp.max(st, axis=0, keepdims=True))
        alpha = jnp.exp2(m_old - m_new)
        p = jnp.exp2((st - m_new).astype(BF16))
        acc_ref[:, cs] = alpha * acc_ref[:, cs] + jnp.dot(
            vt_ref[ki], p, preferred_element_type=F32)
        m_ref[:, cs] = m_new

    def tile(ki, slot, masked, has_next):
        if has_next:
            scores(ki + 1, 0, 1 - slot)
        for c in range(nch):
            if has_next and c + 1 < nch:
                scores(ki + 1, c + 1, 1 - slot)
            softmax_pv(ki, c, slot, masked)

    def run_tiles(k0, n_full, then_diag):
        for i in range(n_full):
            tile(k0 + i, i % 2, False, True)
        if then_diag:
            tile(k0 + n_full, n_full % 2, True, False)

    def group_body(j, carry):
        run_tiles(j * ATTN_GROUP, ATTN_GROUP, False)
        return carry

    for c in range(nch):
        scores(0, c, 0)
    n_groups = qi // ATTN_GROUP
    lax.fori_loop(0, n_groups, group_body, 0)
    for rem in range(ATTN_GROUP):
        @pl.when(qi % ATTN_GROUP == rem)
        def _():
            run_tiles(n_groups * ATTN_GROUP, rem, True)

    lam = (jnp.exp(jnp.sum(lq1_ref[...] * lk1_ref[...], axis=-1, keepdims=True))
           - jnp.exp(jnp.sum(lq2_ref[...] * lk2_ref[...], axis=-1, keepdims=True))
           + LAMBDA_INIT)
    denom = acc_ref[V_HEAD_DIM:V_HEAD_DIM + 1, :]
    o = acc_ref[0:V_HEAD_DIM, :] * (1.0 / denom)
    head = o[:, :t] - lam * o[:, t:]
    ms = jnp.mean(head * head, axis=0, keepdims=True)
    y = head * lax.rsqrt(ms + SUBLN_EPS) * g_ref[...] * (1.0 - LAMBDA_INIT)
    o_ref[...] = y.T.astype(o_ref.dtype)


def _attention(proj3, subln_g, lq1, lk1, lq2, lk2):
    b, s, _ = proj3.shape
    t = ATTN_T
    nq = s // t
    q_blk0 = 2 * D_RNN // V_HEAD_DIM
    k_blk0 = q_blk0 + N_HEADS
    v_blk0 = k_blk0 + N_HEADS
    lvec = pl.BlockSpec((1, HEAD_DIM), lambda bi, h, qi: (0, 0))
    return pl.pallas_call(
        _attn_kernel,
        grid=(b, N_HEADS, nq),
        in_specs=[
            pl.BlockSpec((None, t, V_HEAD_DIM), lambda bi, h, qi: (bi, qi, q_blk0 + h)),
            pl.BlockSpec((None, s, V_HEAD_DIM), lambda bi, h, qi: (bi, 0, k_blk0 + h)),
            pl.BlockSpec((None, s, V_HEAD_DIM), lambda bi, h, qi: (bi, 0, v_blk0 + h)),
            pl.BlockSpec((V_HEAD_DIM, 1), lambda bi, h, qi: (0, 0)),
            lvec, lvec, lvec, lvec,
        ],
        out_specs=pl.BlockSpec((None, t, V_HEAD_DIM), lambda bi, h, qi: (bi, qi, h)),
        out_shape=jax.ShapeDtypeStruct((b, s, D_MODEL), BF16),
        scratch_shapes=[pltpu.VMEM((nq, ATTN_VT_ROWS, t), BF16),
                        pltpu.VMEM((2 * t, V_HEAD_DIM), BF16),
                        pltpu.VMEM((1, 2 * t), F32),
                        pltpu.VMEM((ATTN_VT_ROWS, 2 * t), F32),
                        pltpu.VMEM((2, 2 * t // ATTN_CW, t, ATTN_CW), F32)],
        compiler_params=pltpu.CompilerParams(
            dimension_semantics=("arbitrary", "arbitrary", "arbitrary"),
            vmem_limit_bytes=VMEM_LIMIT_BYTES),
        name="diff_attn",
    )(proj3, proj3, proj3, subln_g, lq1, lk1, lq2, lk2)


def _merge_kernel(x_ref, yr_ref, ya_ref, gr_ref, ga_ref, pa_ref, pb_ref, wo_ref, o_ref):
    a = jnp.dot(yr_ref[...], pa_ref[...], preferred_element_type=F32)
    b = jnp.dot(ya_ref[...], pb_ref[...], preferred_element_type=F32)
    merged = (jax.nn.sigmoid(gr_ref[...].astype(F32)) * a
              + jax.nn.sigmoid(ga_ref[...].astype(F32)) * b)
    o_ref[...] = x_ref[...] + jnp.dot(merged.astype(BF16), wo_ref[...],
                                      preferred_element_type=F32)


def _merge(x2, y_rnn, y_attn, proj, pa, pb, wo):
    t = x2.shape[0]
    tm = MERGE_TM
    row = lambda cb: pl.BlockSpec((tm, D_MODEL), lambda i: (i, cb))
    wspec = pl.BlockSpec((D_MODEL, D_MODEL), lambda i: (0, 0))
    return pl.pallas_call(
        _merge_kernel,
        grid=(t // tm,),
        in_specs=[row(0), row(0), row(0), row(5), row(6), wspec, wspec, wspec],
        out_specs=row(0),
        out_shape=jax.ShapeDtypeStruct((t, D_MODEL), F32),
        compiler_params=pltpu.CompilerParams(
            dimension_semantics=("arbitrary",),
            vmem_limit_bytes=VMEM_LIMIT_BYTES),
        name="merge_out_proj",
    )(x2, y_rnn, y_attn, proj, proj, pa, pb, wo)


def _ffn_kernel(tiles_per_seq, x_ref, halo_ref, g_ref, wup_ref, cw_ref, cb_ref,
                wd_ref, fg_ref, o_ref, h_ref, u_ref, uv_ref, acc_ref):
    i = pl.program_id(0)
    tm = x_ref.shape[0]
    tf = FFN_TF
    nf = D_FF // tf
    lanes = u_ref.shape[-1]

    hh = _rms(halo_ref[...], g_ref[...], EPS)
    hh = jnp.where(i % tiles_per_seq == 0, jnp.zeros_like(hh), hh)
    h_ref[0:HALO, :] = hh.astype(BF16)
    h_ref[HALO:, :] = _rms(x_ref[...], g_ref[...], EPS).astype(BF16)

    def up(c, slot):
        u = jnp.dot(h_ref[...], wup_ref[:, c * tf:(c + 1) * tf],
                    preferred_element_type=F32)
        for s in range(tf // lanes):
            u_ref[slot, s] = u[:, s * lanes:(s + 1) * lanes]
        uv_ref[slot] = jnp.dot(h_ref[HALO:, :], wup_ref[:, D_FF + c * tf:D_FF + (c + 1) * tf],
                               preferred_element_type=F32)

    def act_down(c, slot):
        parts = []
        for s in range(tf // lanes):
            col = slice(c * tf + s * lanes, c * tf + (s + 1) * lanes)
            y = cb_ref[:, col]
            for k in range(FFN_CONV):
                y = y + (cw_ref[FFN_CONV - 1 - k:FFN_CONV - k, col]
                         * u_ref[slot, s, HALO - k:HALO - k + tm, :])
            parts.append((jax.nn.gelu(y) * uv_ref[slot, :, s * lanes:(s + 1) * lanes]
                          ).astype(BF16))
        act = jnp.concatenate(parts, axis=1)
        d = jnp.dot(act, wd_ref[c * tf:(c + 1) * tf, :], preferred_element_type=F32)
        if c == 0:
            acc_ref[...] = d
        else:
            acc_ref[...] += d

    up(0, 0)
    for c in range(nf):
        if c + 1 < nf:
            up(c + 1, (c + 1) % 2)
        act_down(c, c % 2)

    o_ref[...] = _rms(x_ref[...] + acc_ref[...], fg_ref[...], EPS)


def _ffn(x1, g, wup, cw, cb, wd, fg, seq_len):
    t = x1.shape[0]
    tm, tf = FFN_TM, FFN_TF
    hb = tm // HALO
    lanes = 128
    const = lambda shape: pl.BlockSpec(shape, lambda i: (0, 0), pipeline_mode=pl.Buffered(1))
    return pl.pallas_call(
        functools.partial(_ffn_kernel, seq_len // tm),
        grid=(t // tm,),
        in_specs=[
            pl.BlockSpec((tm, D_MODEL), lambda i: (i, 0)),
            pl.BlockSpec((HALO, D_MODEL), lambda i: (jnp.maximum(i * hb - 1, 0), 0)),
            const((1, D_MODEL)),
            const((D_MODEL, 2 * D_FF)),
            const((FFN_CONV, D_FF)),
            const((1, D_FF)),
            const((D_FF, D_MODEL)),
            const((1, D_MODEL)),
        ],
        out_specs=pl.BlockSpec((tm, D_MODEL), lambda i: (i, 0)),
        out_shape=jax.ShapeDtypeStruct((t, D_MODEL), F32),
        scratch_shapes=[pltpu.VMEM((HALO + tm, D_MODEL), BF16),
                        pltpu.VMEM((2, tf // lanes, HALO + tm, lanes), F32),
                        pltpu.VMEM((2, tm, tf), F32),
                        pltpu.VMEM((tm, D_MODEL), F32)],
        compiler_params=pltpu.CompilerParams(
            dimension_semantics=("arbitrary",),
            vmem_limit_bytes=VMEM_LIMIT_BYTES),
        name="conv_ffn",
    )(x1, x1, g, wup, cw, cb, wd, fg)


def kernel(x, attn_norm_g, w_in, rnn_conv_w, rnn_conv_b, rg_wa, rg_ba, rg_wx, rg_bx, rg_lambda, lam_q1, lam_k1, lam_q2, lam_k2, subln_g, w_proj_rnn, w_proj_attn, w_out, mlp_norm_g, w_up, ffn_conv_w, ffn_conv_b, w_down, final_norm_g):
    bsz, s, d = x.shape
    t = bsz * s
    x2 = x.reshape(t, d)
    l = 0
    row = lambda v: v.reshape(1, -1)

    proj = _inproj(x2, row(attn_norm_g[l]), w_in[l].astype(BF16))
    proj3 = proj.reshape(bsz, s, D_IN)

    wcat = (0.5 * jnp.concatenate([rg_wa[l], rg_wx[l]], axis=-1)).astype(BF16)
    y_rnn = _rglru(proj3, rnn_conv_w[l], row(rnn_conv_b[l]), wcat, row(0.5 * rg_ba[l]),
                   row(0.5 * rg_bx[l]), row(rg_lambda[l]))
    y_attn = _attention(proj3, subln_g[l].reshape(V_HEAD_DIM, 1), row(lam_q1[l]),
                        row(lam_k1[l]), row(lam_q2[l]), row(lam_k2[l]))

    x1 = _merge(x2, y_rnn.reshape(t, d), y_attn.reshape(t, d), proj,
                w_proj_rnn[l].astype(BF16), w_proj_attn[l].astype(BF16),
                w_out[l].astype(BF16))
    out = _ffn(x1, row(mlp_norm_g[l]), w_up[l].astype(BF16), ffn_conv_w[l],
               row(ffn_conv_b[l]), w_down[l].astype(BF16), row(final_norm_g), s)
    return out.reshape(bsz, s, d)
```

```python
import functools
import math

import jax
import jax.numpy as jnp
from jax import lax
from jax.experimental import pallas as pl
from jax.experimental.pallas import tpu as pltpu

F32 = jnp.float32
BF16 = jnp.bfloat16

D_MODEL = 1024
EPS = 1e-6
F32_TINY = float(jnp.finfo(jnp.float32).tiny)
SUBLN_EPS = 1e-5
D_RNN = D_MODEL
N_RNN_BLOCKS = 8
RNN_BLOCK = D_RNN // N_RNN_BLOCKS
RNN_CONV = 4
RG_C = 8.0
HEAD_DIM = 64
N_HEADS = D_MODEL // (2 * HEAD_DIM)
V_HEAD_DIM = 2 * HEAD_DIM
ATTN_SCALE = HEAD_DIM ** -0.5
D_FF = 3 * D_MODEL
FFN_CONV = 3
D_IN = 7 * D_MODEL
LAMBDA_INIT = 0.8 - 0.6 * math.exp(-0.3 * 0)

VMEM_LIMIT_BYTES = 56 * 1024 * 1024
HALO = 16

INPROJ_TM = 2048
INPROJ_TN = 1024
RNN_TT = 512
RNN_SCAN_GROUP = 8
ATTN_T = 512
ATTN_CW = 256
ATTN_VT_ROWS = V_HEAD_DIM + 16
ATTN_GROUP = 4
LOG2_E = math.log2(math.e)
MERGE_TM = 512
FFN_TM = 512
FFN_TF = 512


def _rms(x, g, eps):
    return x * lax.rsqrt(jnp.mean(x * x, axis=-1, keepdims=True) + eps) * g


def _inproj_kernel(x_ref, g_ref, w_ref, o_ref, h_ref):
    @pl.when(pl.program_id(1) == 0)
    def _():
        h_ref[...] = _rms(x_ref[...], g_ref[...], EPS).astype(BF16)

    o_ref[...] = jnp.dot(h_ref[...], w_ref[...],
                         preferred_element_type=F32).astype(o_ref.dtype)


def _inproj(x2, g, w):
    t = x2.shape[0]
    tm, tn = INPROJ_TM, INPROJ_TN
    return pl.pallas_call(
        _inproj_kernel,
        grid=(t // tm, D_IN // tn),
        in_specs=[
            pl.BlockSpec((tm, D_MODEL), lambda i, j: (i, 0)),
            pl.BlockSpec((1, D_MODEL), lambda i, j: (0, 0)),
            pl.BlockSpec((D_MODEL, tn), lambda i, j: (0, j)),
        ],
        out_specs=pl.BlockSpec((tm, tn), lambda i, j: (i, j)),
        out_shape=jax.ShapeDtypeStruct((t, D_IN), BF16),
        scratch_shapes=[pltpu.VMEM((tm, D_MODEL), BF16)],
        compiler_params=pltpu.CompilerParams(
            dimension_semantics=("arbitrary", "arbitrary"),
            vmem_limit_bytes=VMEM_LIMIT_BYTES),
        name="in_proj",
    )(x2, g, w)


def _rglru_kernel(xr_ref, gr_ref, halo_ref, cw_ref, cb_ref, wcat_ref, ba_ref, bx_ref,
                  lam_ref, o_ref, carry_ref, ext_ref, a_ref, b_ref, h_ref):
    i = pl.program_id(1)
    tt = xr_ref.shape[0]
    c = D_RNN

    @pl.when(i == 0)
    def _():
        carry_ref[...] = jnp.zeros_like(carry_ref)

    halo = halo_ref[...].astype(F32)
    halo = jnp.where(i == 0, jnp.zeros_like(halo), halo)
    xr = xr_ref[...].astype(F32)
    for n in range(N_RNN_BLOCKS):
        sl = slice(n * RNN_BLOCK, (n + 1) * RNN_BLOCK)
        ext_ref[n, 0:HALO, :] = halo[:, sl]
        ext_ref[n, HALO:, :] = xr[:, sl]

    c4 = (-0.5 * RG_C) * jnp.logaddexp(-lam_ref[...], 0.0)
    for n in range(N_RNN_BLOCKS):
        sl = slice(n * RNN_BLOCK, (n + 1) * RNN_BLOCK)
        xc = cb_ref[:, sl]
        for k in range(RNN_CONV):
            xc = xc + cw_ref[RNN_CONV - 1 - k:RNN_CONV - k, sl] * ext_ref[n, HALO - k:HALO - k + tt, :]
        g = jnp.dot(xc.astype(BF16), wcat_ref[n], preferred_element_type=F32)
        tr = jnp.tanh(g[:, :RNN_BLOCK] + ba_ref[:, sl])
        ti = jnp.tanh(g[:, RNN_BLOCK:] + bx_ref[:, sl])
        log_a = c4[:, sl] * (tr + 1.0)
        a = jnp.exp(log_a)
        y = jnp.maximum(1.0 - a * a, 0.0)
        mult = y * lax.rsqrt(jnp.maximum(y, F32_TINY))
        hx = 0.5 * xc
        a_ref[n] = a
        b_ref[n] = mult * (hx * ti + hx)

    grp = RNN_SCAN_GROUP
    blk = 8 * grp
    rows = lax.broadcasted_iota(jnp.int32, (8, RNN_BLOCK), 0)

    def scan_block(j, carries):
        base = pl.multiple_of(j * blk, blk)
        out = []
        for n in range(N_RNN_BLOCKS):
            cum_a, cum_b = [], []
            for r in range(grp):
                av = a_ref[n, pl.ds(base + r, 8, stride=grp), :]
                bv = b_ref[n, pl.ds(base + r, 8, stride=grp), :]
                if r > 0:
                    bv = av * cum_b[-1] + bv
                    av = av * cum_a[-1]
                cum_a.append(av)
                cum_b.append(bv)
            ga, gb = cum_a[-1], cum_b[-1]
            for d in (1, 2, 4):
                keep = rows >= d
                a_sh = jnp.where(keep, pltpu.roll(ga, d, axis=0), 1.0)
                b_sh = jnp.where(keep, pltpu.roll(gb, d, axis=0), 0.0)
                gb = ga * b_sh + gb
                ga = ga * a_sh
            h_end = ga * carries[n] + gb
            h_in = jnp.where(rows == 0, carries[n], pltpu.roll(h_end, 1, axis=0))
            for r in range(grp):
                h_ref[n, pl.ds(base + r, 8, stride=grp), :] = cum_a[r] * h_in + cum_b[r]
            out.append(jnp.broadcast_to(h_end[7:8, :], (8, RNN_BLOCK)))
        return tuple(out)

    carries = tuple(carry_ref[n] for n in range(N_RNN_BLOCKS))
    carries = lax.fori_loop(0, tt // blk, scan_block, carries, unroll=2)
    for n in range(N_RNN_BLOCKS):
        sl = slice(n * RNN_BLOCK, (n + 1) * RNN_BLOCK)
        carry_ref[n] = carries[n]
        o_ref[:, sl] = (h_ref[n] * jax.nn.gelu(gr_ref[:, sl].astype(F32))).astype(o_ref.dtype)


def _rglru(proj3, cw, cb, wcat, ba, bx, lam):
    b, s, _ = proj3.shape
    tt = RNN_TT
    hb = tt // HALO
    vec = pl.BlockSpec((1, D_RNN), lambda bi, i: (0, 0))
    return pl.pallas_call(
        _rglru_kernel,
        grid=(b, s // tt),
        in_specs=[
            pl.BlockSpec((None, tt, D_RNN), lambda bi, i: (bi, i, 0)),
            pl.BlockSpec((None, tt, D_RNN), lambda bi, i: (bi, i, 1)),
            pl.BlockSpec((None, HALO, D_RNN), lambda bi, i: (bi, jnp.maximum(i * hb - 1, 0), 0)),
            pl.BlockSpec((RNN_CONV, D_RNN), lambda bi, i: (0, 0)),
            vec,
            pl.BlockSpec((N_RNN_BLOCKS, RNN_BLOCK, 2 * RNN_BLOCK), lambda bi, i: (0, 0, 0)),
            vec, vec, vec,
        ],
        out_specs=pl.BlockSpec((None, tt, D_RNN), lambda bi, i: (bi, i, 0)),
        out_shape=jax.ShapeDtypeStruct((b, s, D_RNN), BF16),
        scratch_shapes=[pltpu.VMEM((N_RNN_BLOCKS, 8, RNN_BLOCK), F32),
                        pltpu.VMEM((N_RNN_BLOCKS, HALO + tt, RNN_BLOCK), F32),
                        pltpu.VMEM((N_RNN_BLOCKS, tt, RNN_BLOCK), F32),
                        pltpu.VMEM((N_RNN_BLOCKS, tt, RNN_BLOCK), F32),
                        pltpu.VMEM((N_RNN_BLOCKS, tt, RNN_BLOCK), F32)],
        compiler_params=pltpu.CompilerParams(
            dimension_semantics=("arbitrary", "arbitrary"),
            vmem_limit_bytes=VMEM_LIMIT_BYTES),
        name="rglru",
    )(proj3, proj3, proj3, cw, cb, wcat, ba, bx, lam)


def _attn_kernel(q_ref, qn_ref, k_ref, v_ref, g_ref, lq1_ref, lk1_ref, lq2_ref, lk2_ref,
                 o_ref, vt_ref, qq_ref, qqn_ref, m_ref, acc_ref, st_ref, mx_ref):
    qi = pl.program_id(2)
    t = ATTN_T
    cw = ATTN_CW
    nk = vt_ref.shape[0]
    nch = 2 * t // cw

    def stack_queries(src_ref, dst_ref):
        qs = (src_ref[...].astype(F32) * (ATTN_SCALE * LOG2_E)).astype(BF16)
        lane = lax.broadcasted_iota(jnp.int32, qs.shape, 1)
        zero = jnp.zeros_like(qs)
        dst_ref[0:t, :] = jnp.where(lane < HEAD_DIM, qs, zero)
        dst_ref[t:, :] = jnp.where(lane >= HEAD_DIM, qs, zero)

    def scores(ki, c, slot, queries_ref):
        off = pl.multiple_of(ki * t, t)
        st = lax.dot_general(
            k_ref[pl.ds(off, t), :], queries_ref[c * cw:(c + 1) * cw, :],
            (((1,), (1,)), ((), ())), preferred_element_type=F32)
        st_ref[slot, c] = st
        mx_ref[slot, c] = jnp.max(st, axis=0, keepdims=True)

    stack_queries(q_ref, qq_ref)
    m_ref[...] = jnp.full_like(m_ref, -jnp.inf)
    acc_ref[...] = jnp.zeros_like(acc_ref)

    @pl.when(qi == 0)
    def _():
        ones = jnp.ones((ATTN_VT_ROWS - V_HEAD_DIM, t), BF16)
        for j in range(nk):
            vt_ref[j, 0:V_HEAD_DIM, :] = v_ref[j * t:(j + 1) * t, :].T
            vt_ref[j, V_HEAD_DIM:, :] = ones
        for c in range(nch):
            scores(0, c, 0, qq_ref)

    def softmax_pv(ki, c, slot, masked):
        cs = slice(c * cw, (c + 1) * cw)
        st = st_ref[slot, c]
        if masked:
            r = lax.broadcasted_iota(jnp.int32, st.shape, 0)
            qidx = lax.broadcasted_iota(jnp.int32, st.shape, 1) + (c * cw) % t
            st = jnp.where(r <= qidx, st, -jnp.inf)
            m_loc = jnp.max(st, axis=0, keepdims=True)
        else:
            m_loc = mx_ref[slot, c]
        m_old = m_ref[:, cs]
        m_new = jnp.maximum(m_old, m_loc)
        alpha = jnp.exp2(m_old - m_new)
        p = jnp.exp2(st - m_new).astype(BF16)
        acc_ref[:, cs] = alpha * acc_ref[:, cs] + jnp.dot(
            vt_ref[ki], p, preferred_element_type=F32)
        m_ref[:, cs] = m_new

    def tile(ki, slot, masked, has_next):
        if has_next:
            scores(ki + 1, 0, 1 - slot, qq_ref)
        for c in range(nch):
            if has_next and c + 1 < nch:
                scores(ki + 1, c + 1, 1 - slot, qq_ref)
            softmax_pv(ki, c, slot, masked)

    def run_tiles(k0, n_full, then_diag):
        for i in range(n_full):
            tile(k0 + i, i % 2, False, True)
        if then_diag:
            tile(k0 + n_full, n_full % 2, True, False)

    def group_body(j, carry):
        run_tiles(j * ATTN_GROUP, ATTN_GROUP, False)
        return carry

    n_groups = qi // ATTN_GROUP
    lax.fori_loop(0, n_groups, group_body, 0)
    for rem in range(ATTN_GROUP):
        @pl.when(qi % ATTN_GROUP == rem)
        def _():
            run_tiles(n_groups * ATTN_GROUP, rem, True)

    stack_queries(qn_ref, qqn_ref)
    for c in range(nch):
        scores(0, c, 0, qqn_ref)

    lam = (jnp.exp(jnp.sum(lq1_ref[...] * lk1_ref[...], axis=-1, keepdims=True))
           - jnp.exp(jnp.sum(lq2_ref[...] * lk2_ref[...], axis=-1, keepdims=True))
           + LAMBDA_INIT)
    denom = acc_ref[V_HEAD_DIM:V_HEAD_DIM + 1, :]
    o = acc_ref[0:V_HEAD_DIM, :] * (1.0 / denom)
    head = o[:, :t] - lam * o[:, t:]
    ms = jnp.mean(head * head, axis=0, keepdims=True)
    y = head * lax.rsqrt(ms + SUBLN_EPS) * g_ref[...] * (1.0 - LAMBDA_INIT)
    o_ref[...] = y.T.astype(o_ref.dtype)


def _attention(proj3, subln_g, lq1, lk1, lq2, lk2):
    b, s, _ = proj3.shape
    t = ATTN_T
    nq = s // t
    q_blk0 = 2 * D_RNN // V_HEAD_DIM
    k_blk0 = q_blk0 + N_HEADS
    v_blk0 = k_blk0 + N_HEADS
    lvec = pl.BlockSpec((1, HEAD_DIM), lambda bi, h, qi: (0, 0))
    return pl.pallas_call(
        _attn_kernel,
        grid=(b, N_HEADS, nq),
        in_specs=[
            pl.BlockSpec((None, t, V_HEAD_DIM), lambda bi, h, qi: (bi, qi, q_blk0 + h)),
            pl.BlockSpec((None, t, V_HEAD_DIM),
                         lambda bi, h, qi: (bi, jnp.minimum(qi + 1, nq - 1), q_blk0 + h)),
            pl.BlockSpec((None, s, V_HEAD_DIM), lambda bi, h, qi: (bi, 0, k_blk0 + h)),
            pl.BlockSpec((None, s, V_HEAD_DIM), lambda bi, h, qi: (bi, 0, v_blk0 + h)),
            pl.BlockSpec((V_HEAD_DIM, 1), lambda bi, h, qi: (0, 0)),
            lvec, lvec, lvec, lvec,
        ],
        out_specs=pl.BlockSpec((None, t, V_HEAD_DIM), lambda bi, h, qi: (bi, qi, h)),
        out_shape=jax.ShapeDtypeStruct((b, s, D_MODEL), BF16),
        scratch_shapes=[pltpu.VMEM((nq, ATTN_VT_ROWS, t), BF16),
                        pltpu.VMEM((2 * t, V_HEAD_DIM), BF16),
                        pltpu.VMEM((2 * t, V_HEAD_DIM), BF16),
                        pltpu.VMEM((1, 2 * t), F32),
                        pltpu.VMEM((ATTN_VT_ROWS, 2 * t), F32),
                        pltpu.VMEM((2, 2 * t // ATTN_CW, t, ATTN_CW), F32),
                        pltpu.VMEM((2, 2 * t // ATTN_CW, 1, ATTN_CW), F32)],
        compiler_params=pltpu.CompilerParams(
            dimension_semantics=("arbitrary", "arbitrary", "arbitrary"),
            vmem_limit_bytes=VMEM_LIMIT_BYTES),
        name="diff_attn",
    )(proj3, proj3, proj3, proj3, subln_g, lq1, lk1, lq2, lk2)


def _merge_kernel(x_ref, yr_ref, ya_ref, gr_ref, ga_ref, pa_ref, pb_ref, wo_ref, o_ref):
    a = jnp.dot(yr_ref[...], pa_ref[...], preferred_element_type=F32)
    b = jnp.dot(ya_ref[...], pb_ref[...], preferred_element_type=F32)
    merged = (jax.nn.sigmoid(gr_ref[...].astype(F32)) * a
              + jax.nn.sigmoid(ga_ref[...].astype(F32)) * b)
    o_ref[...] = x_ref[...] + jnp.dot(merged.astype(BF16), wo_ref[...],
                                      preferred_element_type=F32)


def _merge(x2, y_rnn, y_attn, proj, pa, pb, wo):
    t = x2.shape[0]
    tm = MERGE_TM
    row = lambda cb: pl.BlockSpec((tm, D_MODEL), lambda i: (i, cb))
    wspec = pl.BlockSpec((D_MODEL, D_MODEL), lambda i: (0, 0))
    return pl.pallas_call(
        _merge_kernel,
        grid=(t // tm,),
        in_specs=[row(0), row(0), row(0), row(5), row(6), wspec, wspec, wspec],
        out_specs=row(0),
        out_shape=jax.ShapeDtypeStruct((t, D_MODEL), F32),
        compiler_params=pltpu.CompilerParams(
            dimension_semantics=("arbitrary",),
            vmem_limit_bytes=VMEM_LIMIT_BYTES),
        name="merge_out_proj",
    )(x2, y_rnn, y_attn, proj, proj, pa, pb, wo)


def _ffn_kernel(tiles_per_seq, x_ref, halo_ref, g_ref, wup_ref, cw_ref, cb_ref,
                wd_ref, fg_ref, o_ref, h_ref, u_ref, uv_ref, acc_ref):
    i = pl.program_id(0)
    tm = x_ref.shape[0]
    tf = FFN_TF
    nf = D_FF // tf
    lanes = u_ref.shape[-1]

    hh = _rms(halo_ref[...], g_ref[...], EPS)
    hh = jnp.where(i % tiles_per_seq == 0, jnp.zeros_like(hh), hh)
    h_ref[0:HALO, :] = hh.astype(BF16)
    h_ref[HALO:, :] = _rms(x_ref[...], g_ref[...], EPS).astype(BF16)

    def up(c, slot):
        u = jnp.dot(h_ref[...], wup_ref[:, c * tf:(c + 1) * tf],
                    preferred_element_type=F32)
        for s in range(tf // lanes):
            u_ref[slot, s] = u[:, s * lanes:(s + 1) * lanes]
        uv_ref[slot] = jnp.dot(h_ref[HALO:, :], wup_ref[:, D_FF + c * tf:D_FF + (c + 1) * tf],
                               preferred_element_type=F32)

    def act_down(c, slot):
        parts = []
        for s in range(tf // lanes):
            col = slice(c * tf + s * lanes, c * tf + (s + 1) * lanes)
            y = cb_ref[:, col]
            for k in range(FFN_CONV):
                y = y + (cw_ref[FFN_CONV - 1 - k:FFN_CONV - k, col]
                         * u_ref[slot, s, HALO - k:HALO - k + tm, :])
            parts.append((jax.nn.gelu(y) * uv_ref[slot, :, s * lanes:(s + 1) * lanes]
                          ).astype(BF16))
        act = jnp.concatenate(parts, axis=1)
        d = jnp.dot(act, wd_ref[c * tf:(c + 1) * tf, :], preferred_element_type=F32)
        if c == 0:
            acc_ref[...] = d
        else:
            acc_ref[...] += d

    up(0, 0)
    for c in range(nf):
        if c + 1 < nf:
            up(c + 1, (c + 1) % 2)
        act_down(c, c % 2)

    o_ref[...] = _rms(x_ref[...] + acc_ref[...], fg_ref[...], EPS)


def _ffn(x1, g, wup, cw, cb, wd, fg, seq_len):
    t = x1.shape[0]
    tm, tf = FFN_TM, FFN_TF
    hb = tm // HALO
    lanes = 128
    const = lambda shape: pl.BlockSpec(shape, lambda i: (0, 0), pipeline_mode=pl.Buffered(1))
    return pl.pallas_call(
        functools.partial(_ffn_kernel, seq_len // tm),
        grid=(t // tm,),
        in_specs=[
            pl.BlockSpec((tm, D_MODEL), lambda i: (i, 0)),
            pl.BlockSpec((HALO, D_MODEL), lambda i: (jnp.maximum(i * hb - 1, 0), 0)),
            const((1, D_MODEL)),
            const((D_MODEL, 2 * D_FF)),
            const((FFN_CONV, D_FF)),
            const((1, D_FF)),
            const((D_FF, D_MODEL)),
            const((1, D_MODEL)),
        ],
        out_specs=pl.BlockSpec((tm, D_MODEL), lambda i: (i, 0)),
        out_shape=jax.ShapeDtypeStruct((t, D_MODEL), F32),
        scratch_shapes=[pltpu.VMEM((HALO + tm, D_MODEL), BF16),
                        pltpu.VMEM((2, tf // lanes, HALO + tm, lanes), F32),
                        pltpu.VMEM((2, tm, tf), F32),
                        pltpu.VMEM((tm, D_MODEL), F32)],
        compiler_params=pltpu.CompilerParams(
            dimension_semantics=("arbitrary",),
            vmem_limit_bytes=VMEM_LIMIT_BYTES),
        name="conv_ffn",
    )(x1, x1, g, wup, cw, cb, wd, fg)


def kernel(x, attn_norm_g, w_in, rnn_conv_w, rnn_conv_b, rg_wa, rg_ba, rg_wx, rg_bx, rg_lambda, lam_q1, lam_k1, lam_q2, lam_k2, subln_g, w_proj_rnn, w_proj_attn, w_out, mlp_norm_g, w_up, ffn_conv_w, ffn_conv_b, w_down, final_norm_g):
    bsz, s, d = x.shape
    t = bsz * s
    x2 = x.reshape(t, d)
    l = 0
    row = lambda v: v.reshape(1, -1)

    proj = _inproj(x2, row(attn_norm_g[l]), w_in[l].astype(BF16))
    proj3 = proj.reshape(bsz, s, D_IN)

    wcat = (0.5 * jnp.concatenate([rg_wa[l], rg_wx[l]], axis=-1)).astype(BF16)
    y_rnn = _rglru(proj3, rnn_conv_w[l], row(rnn_conv_b[l]), wcat, row(0.5 * rg_ba[l]),
                   row(0.5 * rg_bx[l]), row(rg_lambda[l]))
    y_attn = _attention(proj3, subln_g[l].reshape(V_HEAD_DIM, 1), row(lam_q1[l]),
                        row(lam_k1[l]), row(lam_q2[l]), row(lam_k2[l]))

    x1 = _merge(x2, y_rnn.reshape(t, d), y_attn.reshape(t, d), proj,
                w_proj_rnn[l].astype(BF16), w_proj_attn[l].astype(BF16),
                w_out[l].astype(BF16))
    out = _ffn(x1, row(mlp_norm_g[l]), w_up[l].astype(BF16), ffn_conv_w[l],
               row(ffn_conv_b[l]), w_down[l].astype(BF16), row(final_norm_g), s)
    return out.reshape(bsz, s, d)
```

```python
import functools
import math

import jax
import jax.numpy as jnp
from jax import lax
from jax.experimental import pallas as pl
from jax.experimental.pallas import tpu as pltpu

F32 = jnp.float32
BF16 = jnp.bfloat16

D_MODEL = 1024
EPS = 1e-6
F32_TINY = float(jnp.finfo(jnp.float32).tiny)
SUBLN_EPS = 1e-5
D_RNN = D_MODEL
N_RNN_BLOCKS = 8
RNN_BLOCK = D_RNN // N_RNN_BLOCKS
RNN_CONV = 4
RG_C = 8.0
HEAD_DIM = 64
N_HEADS = D_MODEL // (2 * HEAD_DIM)
V_HEAD_DIM = 2 * HEAD_DIM
ATTN_SCALE = HEAD_DIM ** -0.5
D_FF = 3 * D_MODEL
FFN_CONV = 3
D_IN = 7 * D_MODEL
LAMBDA_INIT = 0.8 - 0.6 * math.exp(-0.3 * 0)

VMEM_LIMIT_BYTES = 56 * 1024 * 1024
HALO = 16

INPROJ_TM = 2048
INPROJ_TN = 1024
RNN_TT = 512
RNN_SCAN_GROUP = 8
ATTN_T = 512
ATTN_CW = 256
ATTN_VT_ROWS = V_HEAD_DIM + 16
ATTN_GROUP = 8
LOG2_E = math.log2(math.e)
MERGE_TM = 512
FFN_TM = 512
FFN_TF = 512
FFN_SLOTS = 3


def _rms(x, g, eps):
    return x * lax.rsqrt(jnp.mean(x * x, axis=-1, keepdims=True) + eps) * g


def _inproj_kernel(x_ref, g_ref, w_ref, o_ref, h_ref):
    @pl.when(pl.program_id(1) == 0)
    def _():
        h_ref[...] = _rms(x_ref[...], g_ref[...], EPS).astype(BF16)

    o_ref[...] = jnp.dot(h_ref[...], w_ref[...],
                         preferred_element_type=F32).astype(o_ref.dtype)


def _inproj(x2, g, w):
    t = x2.shape[0]
    tm, tn = INPROJ_TM, INPROJ_TN
    return pl.pallas_call(
        _inproj_kernel,
        grid=(t // tm, D_IN // tn),
        in_specs=[
            pl.BlockSpec((tm, D_MODEL), lambda i, j: (i, 0)),
            pl.BlockSpec((1, D_MODEL), lambda i, j: (0, 0)),
            pl.BlockSpec((D_MODEL, tn), lambda i, j: (0, j)),
        ],
        out_specs=pl.BlockSpec((tm, tn), lambda i, j: (i, j)),
        out_shape=jax.ShapeDtypeStruct((t, D_IN), BF16),
        scratch_shapes=[pltpu.VMEM((tm, D_MODEL), BF16)],
        compiler_params=pltpu.CompilerParams(
            dimension_semantics=("arbitrary", "arbitrary"),
            vmem_limit_bytes=VMEM_LIMIT_BYTES),
        name="in_proj",
    )(x2, g, w)


def _rglru_kernel(xr_ref, gr_ref, halo_ref, cw_ref, cb_ref, wcat_ref, ba_ref, bx_ref,
                  lam_ref, o_ref, carry_ref, ext_ref, a_ref, b_ref, h_ref):
    i = pl.program_id(1)
    tt = xr_ref.shape[0]
    c = D_RNN

    @pl.when(i == 0)
    def _():
        carry_ref[...] = jnp.zeros_like(carry_ref)

    halo = halo_ref[...].astype(F32)
    halo = jnp.where(i == 0, jnp.zeros_like(halo), halo)
    xr = xr_ref[...].astype(F32)
    for n in range(N_RNN_BLOCKS):
        sl = slice(n * RNN_BLOCK, (n + 1) * RNN_BLOCK)
        ext_ref[n, 0:HALO, :] = halo[:, sl]
        ext_ref[n, HALO:, :] = xr[:, sl]

    c4 = (-0.5 * RG_C) * jnp.logaddexp(-lam_ref[...], 0.0)
    for n in range(N_RNN_BLOCKS):
        sl = slice(n * RNN_BLOCK, (n + 1) * RNN_BLOCK)
        xc = cb_ref[:, sl]
        for k in range(RNN_CONV):
            xc = xc + cw_ref[RNN_CONV - 1 - k:RNN_CONV - k, sl] * ext_ref[n, HALO - k:HALO - k + tt, :]
        g = jnp.dot(xc.astype(BF16), wcat_ref[n], preferred_element_type=F32)
        tr = jnp.tanh(g[:, :RNN_BLOCK] + ba_ref[:, sl])
        ti = jnp.tanh(g[:, RNN_BLOCK:] + bx_ref[:, sl])
        log_a = c4[:, sl] * (tr + 1.0)
        a = jnp.exp(log_a)
        y = jnp.maximum(1.0 - a * a, 0.0)
        mult = y * lax.rsqrt(jnp.maximum(y, F32_TINY))
        hx = 0.5 * xc
        a_ref[n] = a
        b_ref[n] = mult * (hx * ti + hx)

    grp = RNN_SCAN_GROUP
    blk = 8 * grp
    rows = lax.broadcasted_iota(jnp.int32, (8, RNN_BLOCK), 0)

    def scan_block(j, carries):
        base = pl.multiple_of(j * blk, blk)
        out = []
        for n in range(N_RNN_BLOCKS):
            cum_a, cum_b = [], []
            for r in range(grp):
                av = a_ref[n, pl.ds(base + r, 8, stride=grp), :]
                bv = b_ref[n, pl.ds(base + r, 8, stride=grp), :]
                if r > 0:
                    bv = av * cum_b[-1] + bv
                    av = av * cum_a[-1]
                cum_a.append(av)
                cum_b.append(bv)
            ga, gb = cum_a[-1], cum_b[-1]
            for d in (1, 2, 4):
                keep = rows >= d
                a_sh = jnp.where(keep, pltpu.roll(ga, d, axis=0), 1.0)
                b_sh = jnp.where(keep, pltpu.roll(gb, d, axis=0), 0.0)
                gb = ga * b_sh + gb
                ga = ga * a_sh
            h_end = ga * carries[n] + gb
            h_in = jnp.where(rows == 0, carries[n], pltpu.roll(h_end, 1, axis=0))
            for r in range(grp):
                h_ref[n, pl.ds(base + r, 8, stride=grp), :] = cum_a[r] * h_in + cum_b[r]
            out.append(jnp.broadcast_to(h_end[7:8, :], (8, RNN_BLOCK)))
        return tuple(out)

    carries = tuple(carry_ref[n] for n in range(N_RNN_BLOCKS))
    carries = lax.fori_loop(0, tt // blk, scan_block, carries, unroll=2)
    for n in range(N_RNN_BLOCKS):
        sl = slice(n * RNN_BLOCK, (n + 1) * RNN_BLOCK)
        carry_ref[n] = carries[n]
        o_ref[:, sl] = (h_ref[n] * jax.nn.gelu(gr_ref[:, sl].astype(F32))).astype(o_ref.dtype)


def _rglru(proj3, cw, cb, wcat, ba, bx, lam):
    b, s, _ = proj3.shape
    tt = RNN_TT
    hb = tt // HALO
    vec = pl.BlockSpec((1, D_RNN), lambda bi, i: (0, 0))
    return pl.pallas_call(
        _rglru_kernel,
        grid=(b, s // tt),
        in_specs=[
            pl.BlockSpec((None, tt, D_RNN), lambda bi, i: (bi, i, 0)),
            pl.BlockSpec((None, tt, D_RNN), lambda bi, i: (bi, i, 1)),
            pl.BlockSpec((None, HALO, D_RNN), lambda bi, i: (bi, jnp.maximum(i * hb - 1, 0), 0)),
            pl.BlockSpec((RNN_CONV, D_RNN), lambda bi, i: (0, 0)),
            vec,
            pl.BlockSpec((N_RNN_BLOCKS, RNN_BLOCK, 2 * RNN_BLOCK), lambda bi, i: (0, 0, 0)),
            vec, vec, vec,
        ],
        out_specs=pl.BlockSpec((None, tt, D_RNN), lambda bi, i: (bi, i, 0)),
        out_shape=jax.ShapeDtypeStruct((b, s, D_RNN), BF16),
        scratch_shapes=[pltpu.VMEM((N_RNN_BLOCKS, 8, RNN_BLOCK), F32),
                        pltpu.VMEM((N_RNN_BLOCKS, HALO + tt, RNN_BLOCK), F32),
                        pltpu.VMEM((N_RNN_BLOCKS, tt, RNN_BLOCK), F32),
                        pltpu.VMEM((N_RNN_BLOCKS, tt, RNN_BLOCK), F32),
                        pltpu.VMEM((N_RNN_BLOCKS, tt, RNN_BLOCK), F32)],
        compiler_params=pltpu.CompilerParams(
            dimension_semantics=("arbitrary", "arbitrary"),
            vmem_limit_bytes=VMEM_LIMIT_BYTES),
        name="rglru",
    )(proj3, proj3, proj3, cw, cb, wcat, ba, bx, lam)


def _attn_kernel(q_ref, qn_ref, k_ref, v_ref, g_ref, lq1_ref, lk1_ref, lq2_ref, lk2_ref,
                 o_ref, vt_ref, qq_ref, qqn_ref, m_ref, acc_ref, st_ref, mx_ref):
    qi = pl.program_id(2)
    t = ATTN_T
    cw = ATTN_CW
    nk = vt_ref.shape[0]
    nch = 2 * t // cw

    def stack_queries(src_ref, dst_ref):
        qs = (src_ref[...].astype(F32) * (ATTN_SCALE * LOG2_E)).astype(BF16)
        lane = lax.broadcasted_iota(jnp.int32, qs.shape, 1)
        zero = jnp.zeros_like(qs)
        dst_ref[0:t, :] = jnp.where(lane < HEAD_DIM, qs, zero)
        dst_ref[t:, :] = jnp.where(lane >= HEAD_DIM, qs, zero)

    def scores(ki, c, slot, queries_ref):
        off = pl.multiple_of(ki * t, t)
        st = lax.dot_general(
            k_ref[pl.ds(off, t), :], queries_ref[c * cw:(c + 1) * cw, :],
            (((1,), (1,)), ((), ())), preferred_element_type=F32)
        st_ref[slot, c] = st
        mx_ref[slot, c] = jnp.max(st, axis=0, keepdims=True)

    stack_queries(q_ref, qq_ref)
    m_ref[...] = jnp.full_like(m_ref, -jnp.inf)
    acc_ref[...] = jnp.zeros_like(acc_ref)

    @pl.when(qi == 0)
    def _():
        ones = jnp.ones((ATTN_VT_ROWS - V_HEAD_DIM, t), BF16)
        for j in range(nk):
            vt_ref[j, 0:V_HEAD_DIM, :] = v_ref[j * t:(j + 1) * t, :].T
            vt_ref[j, V_HEAD_DIM:, :] = ones
        for c in range(nch):
            scores(0, c, 0, qq_ref)

    def softmax_pv(ki, c, slot, masked):
        cs = slice(c * cw, (c + 1) * cw)
        q0 = (c * cw) % t
        rows = min(t, q0 + cw) if masked else t
        st = st_ref[slot, c, 0:rows, :]
        if masked:
            r = lax.broadcasted_iota(jnp.int32, st.shape, 0)
            qidx = lax.broadcasted_iota(jnp.int32, st.shape, 1) + q0
            st = jnp.where(r <= qidx, st, -jnp.inf)
            m_loc = jnp.max(st, axis=0, keepdims=True)
        else:
            m_loc = mx_ref[slot, c]
        m_old = m_ref[:, cs]
        m_new = jnp.maximum(m_old, m_loc)
        alpha = jnp.exp2(m_old - m_new)
        p = jnp.exp2(st - m_new).astype(BF16)
        acc_ref[:, cs] = alpha * acc_ref[:, cs] + jnp.dot(
            vt_ref[ki, :, 0:rows], p, preferred_element_type=F32)
        m_ref[:, cs] = m_new

    def tile(ki, slot, masked, has_next):
        if has_next:
            scores(ki + 1, 0, 1 - slot, qq_ref)
        for c in range(nch):
            if has_next and c + 1 < nch:
                scores(ki + 1, c + 1, 1 - slot, qq_ref)
            softmax_pv(ki, c, slot, masked)

    def run_tiles(k0, n_full, then_diag):
        for i in range(n_full):
            tile(k0 + i, i % 2, False, True)
        if then_diag:
            tile(k0 + n_full, n_full % 2, True, False)

    def group_body(j, carry):
        run_tiles(j * ATTN_GROUP, ATTN_GROUP, False)
        return carry

    n_groups = qi // ATTN_GROUP
    lax.fori_loop(0, n_groups, group_body, 0)
    for rem in range(ATTN_GROUP):
        @pl.when(qi % ATTN_GROUP == rem)
        def _():
            run_tiles(n_groups * ATTN_GROUP, rem, True)

    stack_queries(qn_ref, qqn_ref)
    for c in range(nch):
        scores(0, c, 0, qqn_ref)

    lam = (jnp.exp(jnp.sum(lq1_ref[...] * lk1_ref[...], axis=-1, keepdims=True))
           - jnp.exp(jnp.sum(lq2_ref[...] * lk2_ref[...], axis=-1, keepdims=True))
           + LAMBDA_INIT)
    denom = acc_ref[V_HEAD_DIM:V_HEAD_DIM + 1, :]
    o = acc_ref[0:V_HEAD_DIM, :] * (1.0 / denom)
    head = o[:, :t] - lam * o[:, t:]
    ms = jnp.mean(head * head, axis=0, keepdims=True)
    y = head * lax.rsqrt(ms + SUBLN_EPS) * g_ref[...] * (1.0 - LAMBDA_INIT)
    o_ref[...] = y.T.astype(o_ref.dtype)


def _attention(proj3, subln_g, lq1, lk1, lq2, lk2):
    b, s, _ = proj3.shape
    t = ATTN_T
    nq = s // t
    q_blk0 = 2 * D_RNN // V_HEAD_DIM
    k_blk0 = q_blk0 + N_HEADS
    v_blk0 = k_blk0 + N_HEADS
    lvec = pl.BlockSpec((1, HEAD_DIM), lambda bi, h, qi: (0, 0))
    return pl.pallas_call(
        _attn_kernel,
        grid=(b, N_HEADS, nq),
        in_specs=[
            pl.BlockSpec((None, t, V_HEAD_DIM), lambda bi, h, qi: (bi, qi, q_blk0 + h)),
            pl.BlockSpec((None, t, V_HEAD_DIM),
                         lambda bi, h, qi: (bi, jnp.minimum(qi + 1, nq - 1), q_blk0 + h)),
            pl.BlockSpec((None, s, V_HEAD_DIM), lambda bi, h, qi: (bi, 0, k_blk0 + h)),
            pl.BlockSpec((None, s, V_HEAD_DIM), lambda bi, h, qi: (bi, 0, v_blk0 + h)),
            pl.BlockSpec((V_HEAD_DIM, 1), lambda bi, h, qi: (0, 0)),
            lvec, lvec, lvec, lvec,
        ],
        out_specs=pl.BlockSpec((None, t, V_HEAD_DIM), lambda bi, h, qi: (bi, qi, h)),
        out_shape=jax.ShapeDtypeStruct((b, s, D_MODEL), BF16),
        scratch_shapes=[pltpu.VMEM((nq, ATTN_VT_ROWS, t), BF16),
                        pltpu.VMEM((2 * t, V_HEAD_DIM), BF16),
                        pltpu.VMEM((2 * t, V_HEAD_DIM), BF16),
                        pltpu.VMEM((1, 2 * t), F32),
                        pltpu.VMEM((ATTN_VT_ROWS, 2 * t), F32),
                        pltpu.VMEM((2, 2 * t // ATTN_CW, t, ATTN_CW), F32),
                        pltpu.VMEM((2, 2 * t // ATTN_CW, 1, ATTN_CW), F32)],
        compiler_params=pltpu.CompilerParams(
            dimension_semantics=("arbitrary", "arbitrary", "arbitrary"),
            vmem_limit_bytes=VMEM_LIMIT_BYTES),
        name="diff_attn",
    )(proj3, proj3, proj3, proj3, subln_g, lq1, lk1, lq2, lk2)


def _merge_kernel(x_ref, yr_ref, ya_ref, gr_ref, ga_ref, pa_ref, pb_ref, wo_ref, o_ref):
    a = jnp.dot(yr_ref[...], pa_ref[...], preferred_element_type=F32)
    b = jnp.dot(ya_ref[...], pb_ref[...], preferred_element_type=F32)
    merged = (jax.nn.sigmoid(gr_ref[...].astype(F32)) * a
              + jax.nn.sigmoid(ga_ref[...].astype(F32)) * b)
    o_ref[...] = x_ref[...] + jnp.dot(merged.astype(BF16), wo_ref[...],
                                      preferred_element_type=F32)


def _merge(x2, y_rnn, y_attn, proj, pa, pb, wo):
    t = x2.shape[0]
    tm = MERGE_TM
    row = lambda cb: pl.BlockSpec((tm, D_MODEL), lambda i: (i, cb))
    wspec = pl.BlockSpec((D_MODEL, D_MODEL), lambda i: (0, 0))
    return pl.pallas_call(
        _merge_kernel,
        grid=(t // tm,),
        in_specs=[row(0), row(0), row(0), row(5), row(6), wspec, wspec, wspec],
        out_specs=row(0),
        out_shape=jax.ShapeDtypeStruct((t, D_MODEL), F32),
        compiler_params=pltpu.CompilerParams(
            dimension_semantics=("arbitrary",),
            vmem_limit_bytes=VMEM_LIMIT_BYTES),
        name="merge_out_proj",
    )(x2, y_rnn, y_attn, proj, proj, pa, pb, wo)


def _ffn_kernel(tiles_per_seq, x_ref, halo_ref, g_ref, wup_ref, cw_ref, cb_ref,
                wd_ref, fg_ref, o_ref, h_ref, u_ref, uv_ref, acc_ref):
    i = pl.program_id(0)
    tm = x_ref.shape[0]
    tf = FFN_TF
    nf = D_FF // tf
    lanes = u_ref.shape[-1]

    hh = _rms(halo_ref[...], g_ref[...], EPS)
    hh = jnp.where(i % tiles_per_seq == 0, jnp.zeros_like(hh), hh)
    h_ref[0:HALO, :] = hh.astype(BF16)
    h_ref[HALO:, :] = _rms(x_ref[...], g_ref[...], EPS).astype(BF16)

    def up(c, slot):
        u = jnp.dot(h_ref[...], wup_ref[:, c * tf:(c + 1) * tf],
                    preferred_element_type=F32)
        for s in range(tf // lanes):
            u_ref[slot, s] = u[:, s * lanes:(s + 1) * lanes]
        uv_ref[slot] = jnp.dot(h_ref[HALO:, :], wup_ref[:, D_FF + c * tf:D_FF + (c + 1) * tf],
                               preferred_element_type=F32)

    def act_down(c, slot):
        parts = []
        for s in range(tf // lanes):
            col = slice(c * tf + s * lanes, c * tf + (s + 1) * lanes)
            y = cb_ref[:, col]
            for k in range(FFN_CONV):
                y = y + (cw_ref[FFN_CONV - 1 - k:FFN_CONV - k, col]
                         * u_ref[slot, s, HALO - k:HALO - k + tm, :])
            parts.append((jax.nn.gelu(y) * uv_ref[slot, :, s * lanes:(s + 1) * lanes]
                          ).astype(BF16))
        act = jnp.concatenate(parts, axis=1)
        d = jnp.dot(act, wd_ref[c * tf:(c + 1) * tf, :], preferred_element_type=F32)
        if c == 0:
            acc_ref[...] = d
        else:
            acc_ref[...] += d

    nslot = u_ref.shape[0]
    for c in range(nslot - 1):
        up(c, c)
    for c in range(nf):
        if c + nslot - 1 < nf:
            up(c + nslot - 1, (c + nslot - 1) % nslot)
        act_down(c, c % nslot)

    o_ref[...] = _rms(x_ref[...] + acc_ref[...], fg_ref[...], EPS)


def _ffn(x1, g, wup, cw, cb, wd, fg, seq_len):
    t = x1.shape[0]
    tm, tf = FFN_TM, FFN_TF
    hb = tm // HALO
    lanes = 128
    const = lambda shape: pl.BlockSpec(shape, lambda i: (0, 0), pipeline_mode=pl.Buffered(1))
    return pl.pallas_call(
        functools.partial(_ffn_kernel, seq_len // tm),
        grid=(t // tm,),
        in_specs=[
            pl.BlockSpec((tm, D_MODEL), lambda i: (i, 0)),
            pl.BlockSpec((HALO, D_MODEL), lambda i: (jnp.maximum(i * hb - 1, 0), 0)),
            const((1, D_MODEL)),
            const((D_MODEL, 2 * D_FF)),
            const((FFN_CONV, D_FF)),
            const((1, D_FF)),
            const((D_FF, D_MODEL)),
            const((1, D_MODEL)),
        ],
        out_specs=pl.BlockSpec((tm, D_MODEL), lambda i: (i, 0)),
        out_shape=jax.ShapeDtypeStruct((t, D_MODEL), F32),
        scratch_shapes=[pltpu.VMEM((HALO + tm, D_MODEL), BF16),
                        pltpu.VMEM((FFN_SLOTS, tf // lanes, HALO + tm, lanes), F32),
                        pltpu.VMEM((FFN_SLOTS, tm, tf), F32),
                        pltpu.VMEM((tm, D_MODEL), F32)],
        compiler_params=pltpu.CompilerParams(
            dimension_semantics=("arbitrary",),
            vmem_limit_bytes=VMEM_LIMIT_BYTES),
        name="conv_ffn",
    )(x1, x1, g, wup, cw, cb, wd, fg)


def kernel(x, attn_norm_g, w_in, rnn_conv_w, rnn_conv_b, rg_wa, rg_ba, rg_wx, rg_bx, rg_lambda, lam_q1, lam_k1, lam_q2, lam_k2, subln_g, w_proj_rnn, w_proj_attn, w_out, mlp_norm_g, w_up, ffn_conv_w, ffn_conv_b, w_down, final_norm_g):
    bsz, s, d = x.shape
    t = bsz * s
    x2 = x.reshape(t, d)
    l = 0
    row = lambda v: v.reshape(1, -1)

    proj = _inproj(x2, row(attn_norm_g[l]), w_in[l].astype(BF16))
    proj3 = proj.reshape(bsz, s, D_IN)

    wcat = (0.5 * jnp.concatenate([rg_wa[l], rg_wx[l]], axis=-1)).astype(BF16)
    y_rnn = _rglru(proj3, rnn_conv_w[l], row(rnn_conv_b[l]), wcat, row(0.5 * rg_ba[l]),
                   row(0.5 * rg_bx[l]), row(rg_lambda[l]))
    y_attn = _attention(proj3, subln_g[l].reshape(V_HEAD_DIM, 1), row(lam_q1[l]),
                        row(lam_k1[l]), row(lam_q2[l]), row(lam_k2[l]))

    x1 = _merge(x2, y_rnn.reshape(t, d), y_attn.reshape(t, d), proj,
                w_proj_rnn[l].astype(BF16), w_proj_attn[l].astype(BF16),
                w_out[l].astype(BF16))
    out = _ffn(x1, row(mlp_norm_g[l]), w_up[l].astype(BF16), ffn_conv_w[l],
               row(ffn_conv_b[l]), w_down[l].astype(BF16), row(final_norm_g), s)
    return out.reshape(bsz, s, d)
```

```python
import functools
import math

import jax
import jax.numpy as jnp
from jax import lax
from jax.experimental import pallas as pl
from jax.experimental.pallas import tpu as pltpu

F32 = jnp.float32
BF16 = jnp.bfloat16

D_MODEL = 1024
EPS = 1e-6
F32_TINY = float(jnp.finfo(jnp.float32).tiny)
SUBLN_EPS = 1e-5
D_RNN = D_MODEL
N_RNN_BLOCKS = 8
RNN_BLOCK = D_RNN // N_RNN_BLOCKS
RNN_CONV = 4
RG_C = 8.0
HEAD_DIM = 64
N_HEADS = D_MODEL // (2 * HEAD_DIM)
V_HEAD_DIM = 2 * HEAD_DIM
ATTN_SCALE = HEAD_DIM ** -0.5
D_FF = 3 * D_MODEL
FFN_CONV = 3
D_IN = 7 * D_MODEL
LAMBDA_INIT = 0.8 - 0.6 * math.exp(-0.3 * 0)

VMEM_LIMIT_BYTES = 56 * 1024 * 1024
HALO = 16

INPROJ_TM = 2048
INPROJ_TN = 1024
RNN_TT = 512
RNN_SCAN_GROUP = 8
ATTN_T = 512
ATTN_CW = 256
ATTN_VT_ROWS = V_HEAD_DIM + 16
ATTN_GROUP = 8
ATTN_HEADS = 2
LOG2_E = math.log2(math.e)
MERGE_TM = 512
FFN_TM = 512
FFN_TF = 512
FFN_SLOTS = 3


def _rms(x, g, eps):
    return x * lax.rsqrt(jnp.mean(x * x, axis=-1, keepdims=True) + eps) * g


def _inproj_kernel(x_ref, g_ref, w_ref, o_ref, h_ref):
    @pl.when(pl.program_id(1) == 0)
    def _():
        h_ref[...] = _rms(x_ref[...], g_ref[...], EPS).astype(BF16)

    o_ref[...] = jnp.dot(h_ref[...], w_ref[...],
                         preferred_element_type=F32).astype(o_ref.dtype)


def _inproj(x2, g, w):
    t = x2.shape[0]
    tm, tn = INPROJ_TM, INPROJ_TN
    return pl.pallas_call(
        _inproj_kernel,
        grid=(t // tm, D_IN // tn),
        in_specs=[
            pl.BlockSpec((tm, D_MODEL), lambda i, j: (i, 0)),
            pl.BlockSpec((1, D_MODEL), lambda i, j: (0, 0)),
            pl.BlockSpec((D_MODEL, tn), lambda i, j: (0, j)),
        ],
        out_specs=pl.BlockSpec((tm, tn), lambda i, j: (i, j)),
        out_shape=jax.ShapeDtypeStruct((t, D_IN), BF16),
        scratch_shapes=[pltpu.VMEM((tm, D_MODEL), BF16)],
        compiler_params=pltpu.CompilerParams(
            dimension_semantics=("arbitrary", "arbitrary"),
            vmem_limit_bytes=VMEM_LIMIT_BYTES),
        name="in_proj",
    )(x2, g, w)


def _rglru_kernel(xr_ref, gr_ref, halo_ref, cw_ref, cb_ref, wcat_ref, ba_ref, bx_ref,
                  lam_ref, o_ref, carry_ref, ext_ref, a_ref, b_ref, h_ref):
    i = pl.program_id(1)
    tt = xr_ref.shape[0]
    c = D_RNN

    @pl.when(i == 0)
    def _():
        carry_ref[...] = jnp.zeros_like(carry_ref)

    halo = halo_ref[...].astype(F32)
    halo = jnp.where(i == 0, jnp.zeros_like(halo), halo)
    xr = xr_ref[...].astype(F32)
    for n in range(N_RNN_BLOCKS):
        sl = slice(n * RNN_BLOCK, (n + 1) * RNN_BLOCK)
        ext_ref[n, 0:HALO, :] = halo[:, sl]
        ext_ref[n, HALO:, :] = xr[:, sl]

    c4 = (-0.5 * RG_C) * jnp.logaddexp(-lam_ref[...], 0.0)
    for n in range(N_RNN_BLOCKS):
        sl = slice(n * RNN_BLOCK, (n + 1) * RNN_BLOCK)
        xc = cb_ref[:, sl]
        for k in range(RNN_CONV):
            xc = xc + cw_ref[RNN_CONV - 1 - k:RNN_CONV - k, sl] * ext_ref[n, HALO - k:HALO - k + tt, :]
        g = jnp.dot(xc.astype(BF16), wcat_ref[n], preferred_element_type=F32)
        tr = jnp.tanh(g[:, :RNN_BLOCK] + ba_ref[:, sl])
        ti = jnp.tanh(g[:, RNN_BLOCK:] + bx_ref[:, sl])
        log_a = c4[:, sl] * (tr + 1.0)
        a = jnp.exp(log_a)
        y = jnp.maximum(1.0 - a * a, 0.0)
        mult = y * lax.rsqrt(jnp.maximum(y, F32_TINY))
        hx = 0.5 * xc
        a_ref[n] = a
        b_ref[n] = mult * (hx * ti + hx)

    grp = RNN_SCAN_GROUP
    blk = 8 * grp
    rows = lax.broadcasted_iota(jnp.int32, (8, RNN_BLOCK), 0)

    def scan_block(j, carries):
        base = pl.multiple_of(j * blk, blk)
        out = []
        for n in range(N_RNN_BLOCKS):
            cum_a, cum_b = [], []
            for r in range(grp):
                av = a_ref[n, pl.ds(base + r, 8, stride=grp), :]
                bv = b_ref[n, pl.ds(base + r, 8, stride=grp), :]
                if r > 0:
                    bv = av * cum_b[-1] + bv
                    av = av * cum_a[-1]
                cum_a.append(av)
                cum_b.append(bv)
            ga, gb = cum_a[-1], cum_b[-1]
            for d in (1, 2, 4):
                keep = rows >= d
                a_sh = jnp.where(keep, pltpu.roll(ga, d, axis=0), 1.0)
                b_sh = jnp.where(keep, pltpu.roll(gb, d, axis=0), 0.0)
                gb = ga * b_sh + gb
                ga = ga * a_sh
            h_end = ga * carries[n] + gb
            h_in = jnp.where(rows == 0, carries[n], pltpu.roll(h_end, 1, axis=0))
            for r in range(grp):
                h_ref[n, pl.ds(base + r, 8, stride=grp), :] = cum_a[r] * h_in + cum_b[r]
            out.append(jnp.broadcast_to(h_end[7:8, :], (8, RNN_BLOCK)))
        return tuple(out)

    carries = tuple(carry_ref[n] for n in range(N_RNN_BLOCKS))
    carries = lax.fori_loop(0, tt // blk, scan_block, carries, unroll=2)
    for n in range(N_RNN_BLOCKS):
        sl = slice(n * RNN_BLOCK, (n + 1) * RNN_BLOCK)
        carry_ref[n] = carries[n]
        o_ref[:, sl] = (h_ref[n] * jax.nn.gelu(gr_ref[:, sl].astype(F32))).astype(o_ref.dtype)


def _rglru(proj3, cw, cb, wcat, ba, bx, lam):
    b, s, _ = proj3.shape
    tt = RNN_TT
    hb = tt // HALO
    vec = pl.BlockSpec((1, D_RNN), lambda bi, i: (0, 0))
    return pl.pallas_call(
        _rglru_kernel,
        grid=(b, s // tt),
        in_specs=[
            pl.BlockSpec((None, tt, D_RNN), lambda bi, i: (bi, i, 0)),
            pl.BlockSpec((None, tt, D_RNN), lambda bi, i: (bi, i, 1)),
            pl.BlockSpec((None, HALO, D_RNN), lambda bi, i: (bi, jnp.maximum(i * hb - 1, 0), 0)),
            pl.BlockSpec((RNN_CONV, D_RNN), lambda bi, i: (0, 0)),
            vec,
            pl.BlockSpec((N_RNN_BLOCKS, RNN_BLOCK, 2 * RNN_BLOCK), lambda bi, i: (0, 0, 0)),
            vec, vec, vec,
        ],
        out_specs=pl.BlockSpec((None, tt, D_RNN), lambda bi, i: (bi, i, 0)),
        out_shape=jax.ShapeDtypeStruct((b, s, D_RNN), BF16),
        scratch_shapes=[pltpu.VMEM((N_RNN_BLOCKS, 8, RNN_BLOCK), F32),
                        pltpu.VMEM((N_RNN_BLOCKS, HALO + tt, RNN_BLOCK), F32),
                        pltpu.VMEM((N_RNN_BLOCKS, tt, RNN_BLOCK), F32),
                        pltpu.VMEM((N_RNN_BLOCKS, tt, RNN_BLOCK), F32),
                        pltpu.VMEM((N_RNN_BLOCKS, tt, RNN_BLOCK), F32)],
        compiler_params=pltpu.CompilerParams(
            dimension_semantics=("arbitrary", "arbitrary"),
            vmem_limit_bytes=VMEM_LIMIT_BYTES),
        name="rglru",
    )(proj3, proj3, proj3, cw, cb, wcat, ba, bx, lam)


def _attn_kernel(q_ref, qn_ref, k_ref, v_ref, g_ref, lq1_ref, lk1_ref, lq2_ref, lk2_ref,
                 o_ref, vt_ref, qq_ref, qqn_ref, m_ref, acc_ref, st_ref, mx_ref):
    qi = pl.program_id(2)
    t = ATTN_T
    cw = ATTN_CW
    nk = vt_ref.shape[1]
    nch = 2 * t // cw
    heads = range(ATTN_HEADS)

    def head_cols(hh):
        return slice(hh * V_HEAD_DIM, (hh + 1) * V_HEAD_DIM)

    def stack_queries(src_ref, dst_ref):
        for hh in heads:
            qs = (src_ref[:, head_cols(hh)].astype(F32) * (ATTN_SCALE * LOG2_E)).astype(BF16)
            lane = lax.broadcasted_iota(jnp.int32, qs.shape, 1)
            zero = jnp.zeros_like(qs)
            dst_ref[hh, 0:t, :] = jnp.where(lane < HEAD_DIM, qs, zero)
            dst_ref[hh, t:, :] = jnp.where(lane >= HEAD_DIM, qs, zero)

    def scores(ki, hh, c, slot, queries_ref):
        off = pl.multiple_of(ki * t, t)
        st = lax.dot_general(
            k_ref[pl.ds(off, t), head_cols(hh)], queries_ref[hh, c * cw:(c + 1) * cw, :],
            (((1,), (1,)), ((), ())), preferred_element_type=F32)
        st_ref[slot, hh, c] = st
        mx_ref[slot, hh, c] = jnp.max(st, axis=0, keepdims=True)

    stack_queries(q_ref, qq_ref)
    m_ref[...] = jnp.full_like(m_ref, -jnp.inf)
    acc_ref[...] = jnp.zeros_like(acc_ref)

    @pl.when(qi == 0)
    def _():
        ones = jnp.ones((ATTN_VT_ROWS - V_HEAD_DIM, t), BF16)
        for hh in heads:
            for j in range(nk):
                vt_ref[hh, j, 0:V_HEAD_DIM, :] = v_ref[j * t:(j + 1) * t, head_cols(hh)].T
                vt_ref[hh, j, V_HEAD_DIM:, :] = ones
        for c in range(nch):
            for hh in heads:
                scores(0, hh, c, 0, qq_ref)

    def softmax_pv(ki, hh, c, slot, masked):
        cs = slice(c * cw, (c + 1) * cw)
        q0 = (c * cw) % t
        rows = min(t, q0 + cw) if masked else t
        st = st_ref[slot, hh, c, 0:rows, :]
        if masked:
            r = lax.broadcasted_iota(jnp.int32, st.shape, 0)
            qidx = lax.broadcasted_iota(jnp.int32, st.shape, 1) + q0
            st = jnp.where(r <= qidx, st, -jnp.inf)
            m_loc = jnp.max(st, axis=0, keepdims=True)
        else:
            m_loc = mx_ref[slot, hh, c]
        m_old = m_ref[hh, :, cs]
        m_new = jnp.maximum(m_old, m_loc)
        alpha = jnp.exp2(m_old - m_new)
        p = jnp.exp2(st - m_new).astype(BF16)
        acc_ref[hh, :, cs] = alpha * acc_ref[hh, :, cs] + jnp.dot(
            vt_ref[hh, ki, :, 0:rows], p, preferred_element_type=F32)
        m_ref[hh, :, cs] = m_new

    def tile(ki, slot, masked, has_next):
        if has_next:
            for hh in heads:
                scores(ki + 1, hh, 0, 1 - slot, qq_ref)
        for c in range(nch):
            for hh in heads:
                if has_next and c + 1 < nch:
                    scores(ki + 1, hh, c + 1, 1 - slot, qq_ref)
                softmax_pv(ki, hh, c, slot, masked)

    def run_tiles(k0, n_full, then_diag):
        for i in range(n_full):
            tile(k0 + i, i % 2, False, True)
        if then_diag:
            tile(k0 + n_full, n_full % 2, True, False)

    def group_body(j, carry):
        run_tiles(j * ATTN_GROUP, ATTN_GROUP, False)
        return carry

    n_groups = qi // ATTN_GROUP
    lax.fori_loop(0, n_groups, group_body, 0)
    for rem in range(ATTN_GROUP):
        @pl.when(qi % ATTN_GROUP == rem)
        def _():
            run_tiles(n_groups * ATTN_GROUP, rem, True)

    stack_queries(qn_ref, qqn_ref)
    for c in range(nch):
        for hh in heads:
            scores(0, hh, c, 0, qqn_ref)

    lam = (jnp.exp(jnp.sum(lq1_ref[...] * lk1_ref[...], axis=-1, keepdims=True))
           - jnp.exp(jnp.sum(lq2_ref[...] * lk2_ref[...], axis=-1, keepdims=True))
           + LAMBDA_INIT)
    for hh in heads:
        denom = acc_ref[hh, V_HEAD_DIM:V_HEAD_DIM + 1, :]
        o = acc_ref[hh, 0:V_HEAD_DIM, :] * (1.0 / denom)
        head = o[:, :t] - lam * o[:, t:]
        ms = jnp.mean(head * head, axis=0, keepdims=True)
        y = head * lax.rsqrt(ms + SUBLN_EPS) * g_ref[...] * (1.0 - LAMBDA_INIT)
        o_ref[:, head_cols(hh)] = y.T.astype(o_ref.dtype)


def _attention(proj3, subln_g, lq1, lk1, lq2, lk2):
    b, s, _ = proj3.shape
    t = ATTN_T
    nq = s // t
    width = ATTN_HEADS * V_HEAD_DIM
    q_blk0 = 2 * D_RNN // width
    k_blk0 = q_blk0 + D_MODEL // width
    v_blk0 = k_blk0 + D_MODEL // width
    lvec = pl.BlockSpec((1, HEAD_DIM), lambda bi, h, qi: (0, 0))
    nch = 2 * t // ATTN_CW
    return pl.pallas_call(
        _attn_kernel,
        grid=(b, N_HEADS // ATTN_HEADS, nq),
        in_specs=[
            pl.BlockSpec((None, t, width), lambda bi, h, qi: (bi, qi, q_blk0 + h)),
            pl.BlockSpec((None, t, width),
                         lambda bi, h, qi: (bi, jnp.minimum(qi + 1, nq - 1), q_blk0 + h)),
            pl.BlockSpec((None, s, width), lambda bi, h, qi: (bi, 0, k_blk0 + h)),
            pl.BlockSpec((None, s, width), lambda bi, h, qi: (bi, 0, v_blk0 + h)),
            pl.BlockSpec((V_HEAD_DIM, 1), lambda bi, h, qi: (0, 0)),
            lvec, lvec, lvec, lvec,
        ],
        out_specs=pl.BlockSpec((None, t, width), lambda bi, h, qi: (bi, qi, h)),
        out_shape=jax.ShapeDtypeStruct((b, s, D_MODEL), BF16),
        scratch_shapes=[pltpu.VMEM((ATTN_HEADS, nq, ATTN_VT_ROWS, t), BF16),
                        pltpu.VMEM((ATTN_HEADS, 2 * t, V_HEAD_DIM), BF16),
                        pltpu.VMEM((ATTN_HEADS, 2 * t, V_HEAD_DIM), BF16),
                        pltpu.VMEM((ATTN_HEADS, 1, 2 * t), F32),
                        pltpu.VMEM((ATTN_HEADS, ATTN_VT_ROWS, 2 * t), F32),
                        pltpu.VMEM((2, ATTN_HEADS, nch, t, ATTN_CW), F32),
                        pltpu.VMEM((2, ATTN_HEADS, nch, 1, ATTN_CW), F32)],
        compiler_params=pltpu.CompilerParams(
            dimension_semantics=("arbitrary", "arbitrary", "arbitrary"),
            vmem_limit_bytes=VMEM_LIMIT_BYTES),
        name="diff_attn",
    )(proj3, proj3, proj3, proj3, subln_g, lq1, lk1, lq2, lk2)


def _merge_kernel(x_ref, yr_ref, ya_ref, gr_ref, ga_ref, pa_ref, pb_ref, wo_ref, o_ref):
    a = jnp.dot(yr_ref[...], pa_ref[...], preferred_element_type=F32)
    b = jnp.dot(ya_ref[...], pb_ref[...], preferred_element_type=F32)
    merged = (jax.nn.sigmoid(gr_ref[...].astype(F32)) * a
              + jax.nn.sigmoid(ga_ref[...].astype(F32)) * b)
    o_ref[...] = x_ref[...] + jnp.dot(merged.astype(BF16), wo_ref[...],
                                      preferred_element_type=F32)


def _merge(x2, y_rnn, y_attn, proj, pa, pb, wo):
    t = x2.shape[0]
    tm = MERGE_TM
    row = lambda cb: pl.BlockSpec((tm, D_MODEL), lambda i: (i, cb))
    wspec = pl.BlockSpec((D_MODEL, D_MODEL), lambda i: (0, 0))
    return pl.pallas_call(
        _merge_kernel,
        grid=(t // tm,),
        in_specs=[row(0), row(0), row(0), row(5), row(6), wspec, wspec, wspec],
        out_specs=row(0),
        out_shape=jax.ShapeDtypeStruct((t, D_MODEL), F32),
        compiler_params=pltpu.CompilerParams(
            dimension_semantics=("arbitrary",),
            vmem_limit_bytes=VMEM_LIMIT_BYTES),
        name="merge_out_proj",
    )(x2, y_rnn, y_attn, proj, proj, pa, pb, wo)


def _ffn_kernel(tiles_per_seq, x_ref, halo_ref, g_ref, wup_ref, cw_ref, cb_ref,
                wd_ref, fg_ref, o_ref, h_ref, u_ref, uv_ref, acc_ref):
    i = pl.program_id(0)
    tm = x_ref.shape[0]
    tf = FFN_TF
    nf = D_FF // tf
    lanes = u_ref.shape[-1]

    hh = _rms(halo_ref[...], g_ref[...], EPS)
    hh = jnp.where(i % tiles_per_seq == 0, jnp.zeros_like(hh), hh)
    h_ref[0:HALO, :] = hh.astype(BF16)
    h_ref[HALO:, :] = _rms(x_ref[...], g_ref[...], EPS).astype(BF16)

    def up(c, slot):
        u = jnp.dot(h_ref[...], wup_ref[:, c * tf:(c + 1) * tf],
                    preferred_element_type=F32)
        for s in range(tf // lanes):
            u_ref[slot, s] = u[:, s * lanes:(s + 1) * lanes]
        uv_ref[slot] = jnp.dot(h_ref[HALO:, :], wup_ref[:, D_FF + c * tf:D_FF + (c + 1) * tf],
                               preferred_element_type=F32)

    def act_down(c, slot):
        parts = []
        for s in range(tf // lanes):
            col = slice(c * tf + s * lanes, c * tf + (s + 1) * lanes)
            y = cb_ref[:, col]
            for k in range(FFN_CONV):
                y = y + (cw_ref[FFN_CONV - 1 - k:FFN_CONV - k, col]
                         * u_ref[slot, s, HALO - k:HALO - k + tm, :])
            parts.append((jax.nn.gelu(y) * uv_ref[slot, :, s * lanes:(s + 1) * lanes]
                          ).astype(BF16))
        act = jnp.concatenate(parts, axis=1)
        d = jnp.dot(act, wd_ref[c * tf:(c + 1) * tf, :], preferred_element_type=F32)
        if c == 0:
            acc_ref[...] = d
        else:
            acc_ref[...] += d

    nslot = u_ref.shape[0]
    for c in range(nslot - 1):
        up(c, c)
    for c in range(nf):
        if c + nslot - 1 < nf:
            up(c + nslot - 1, (c + nslot - 1) % nslot)
        act_down(c, c % nslot)

    o_ref[...] = _rms(x_ref[...] + acc_ref[...], fg_ref[...], EPS)


def _ffn(x1, g, wup, cw, cb, wd, fg, seq_len):
    t = x1.shape[0]
    tm, tf = FFN_TM, FFN_TF
    hb = tm // HALO
    lanes = 128
    const = lambda shape: pl.BlockSpec(shape, lambda i: (0, 0), pipeline_mode=pl.Buffered(1))
    return pl.pallas_call(
        functools.partial(_ffn_kernel, seq_len // tm),
        grid=(t // tm,),
        in_specs=[
            pl.BlockSpec((tm, D_MODEL), lambda i: (i, 0)),
            pl.BlockSpec((HALO, D_MODEL), lambda i: (jnp.maximum(i * hb - 1, 0), 0)),
            const((1, D_MODEL)),
            const((D_MODEL, 2 * D_FF)),
            const((FFN_CONV, D_FF)),
            const((1, D_FF)),
            const((D_FF, D_MODEL)),
            const((1, D_MODEL)),
        ],
        out_specs=pl.BlockSpec((tm, D_MODEL), lambda i: (i, 0)),
        out_shape=jax.ShapeDtypeStruct((t, D_MODEL), F32),
        scratch_shapes=[pltpu.VMEM((HALO + tm, D_MODEL), BF16),
                        pltpu.VMEM((FFN_SLOTS, tf // lanes, HALO + tm, lanes), F32),
                        pltpu.VMEM((FFN_SLOTS, tm, tf), F32),
                        pltpu.VMEM((tm, D_MODEL), F32)],
        compiler_params=pltpu.CompilerParams(
            dimension_semantics=("arbitrary",),
            vmem_limit_bytes=VMEM_LIMIT_BYTES),
        name="conv_ffn",
    )(x1, x1, g, wup, cw, cb, wd, fg)


def kernel(x, attn_norm_g, w_in, rnn_conv_w, rnn_conv_b, rg_wa, rg_ba, rg_wx, rg_bx, rg_lambda, lam_q1, lam_k1, lam_q2, lam_k2, subln_g, w_proj_rnn, w_proj_attn, w_out, mlp_norm_g, w_up, ffn_conv_w, ffn_conv_b, w_down, final_norm_g):
    bsz, s, d = x.shape
    t = bsz * s
    x2 = x.reshape(t, d)
    l = 0
    row = lambda v: v.reshape(1, -1)

    proj = _inproj(x2, row(attn_norm_g[l]), w_in[l].astype(BF16))
    proj3 = proj.reshape(bsz, s, D_IN)

    wcat = (0.5 * jnp.concatenate([rg_wa[l], rg_wx[l]], axis=-1)).astype(BF16)
    y_rnn = _rglru(proj3, rnn_conv_w[l], row(rnn_conv_b[l]), wcat, row(0.5 * rg_ba[l]),
                   row(0.5 * rg_bx[l]), row(rg_lambda[l]))
    y_attn = _attention(proj3, subln_g[l].reshape(V_HEAD_DIM, 1), row(lam_q1[l]),
                        row(lam_k1[l]), row(lam_q2[l]), row(lam_k2[l]))

    x1 = _merge(x2, y_rnn.reshape(t, d), y_attn.reshape(t, d), proj,
                w_proj_rnn[l].astype(BF16), w_proj_attn[l].astype(BF16),
                w_out[l].astype(BF16))
    out = _ffn(x1, row(mlp_norm_g[l]), w_up[l].astype(BF16), ffn_conv_w[l],
               row(ffn_conv_b[l]), w_down[l].astype(BF16), row(final_norm_g), s)
    return out.reshape(bsz, s, d)
```

```python
import functools
import math

import jax
import jax.numpy as jnp
from jax import lax
from jax.experimental import pallas as pl
from jax.experimental.pallas import tpu as pltpu

F32 = jnp.float32
BF16 = jnp.bfloat16

D_MODEL = 1024
EPS = 1e-6
F32_TINY = float(jnp.finfo(jnp.float32).tiny)
SUBLN_EPS = 1e-5
D_RNN = D_MODEL
N_RNN_BLOCKS = 8
RNN_BLOCK = D_RNN // N_RNN_BLOCKS
RNN_CONV = 4
RG_C = 8.0
HEAD_DIM = 64
N_HEADS = D_MODEL // (2 * HEAD_DIM)
V_HEAD_DIM = 2 * HEAD_DIM
ATTN_SCALE = HEAD_DIM ** -0.5
D_FF = 3 * D_MODEL
FFN_CONV = 3
D_IN = 7 * D_MODEL
LAMBDA_INIT = 0.8 - 0.6 * math.exp(-0.3 * 0)

VMEM_LIMIT_BYTES = 56 * 1024 * 1024
HALO = 16

INPROJ_TM = 2048
INPROJ_TN = 1024
RNN_TT = 512
RNN_SCAN_GROUP = 8
ATTN_T = 512
ATTN_CW = 256
ATTN_VT_ROWS = V_HEAD_DIM + 16
ATTN_GROUP = 4
ATTN_HEADS = 2
LOG2_E = math.log2(math.e)
MERGE_TM = 512
FFN_TM = 512
FFN_TF = 512
FFN_SLOTS = 3


def _rms(x, g, eps):
    return x * lax.rsqrt(jnp.mean(x * x, axis=-1, keepdims=True) + eps) * g


def _inproj_kernel(x_ref, g_ref, w_ref, o_ref, h_ref):
    @pl.when(pl.program_id(1) == 0)
    def _():
        h_ref[...] = _rms(x_ref[...], g_ref[...], EPS).astype(BF16)

    o_ref[...] = jnp.dot(h_ref[...], w_ref[...],
                         preferred_element_type=F32).astype(o_ref.dtype)


def _inproj(x2, g, w):
    t = x2.shape[0]
    tm, tn = INPROJ_TM, INPROJ_TN
    return pl.pallas_call(
        _inproj_kernel,
        grid=(t // tm, D_IN // tn),
        in_specs=[
            pl.BlockSpec((tm, D_MODEL), lambda i, j: (i, 0)),
            pl.BlockSpec((1, D_MODEL), lambda i, j: (0, 0)),
            pl.BlockSpec((D_MODEL, tn), lambda i, j: (0, j)),
        ],
        out_specs=pl.BlockSpec((tm, tn), lambda i, j: (i, j)),
        out_shape=jax.ShapeDtypeStruct((t, D_IN), BF16),
        scratch_shapes=[pltpu.VMEM((tm, D_MODEL), BF16)],
        compiler_params=pltpu.CompilerParams(
            dimension_semantics=("arbitrary", "arbitrary"),
            vmem_limit_bytes=VMEM_LIMIT_BYTES),
        name="in_proj",
    )(x2, g, w)


def _rglru_kernel(xr_ref, gr_ref, halo_ref, cw_ref, cb_ref, wcat_ref, ba_ref, bx_ref,
                  lam_ref, o_ref, carry_ref, ext_ref, a_ref, b_ref, h_ref):
    i = pl.program_id(1)
    tt = xr_ref.shape[0]
    c = D_RNN

    @pl.when(i == 0)
    def _():
        carry_ref[...] = jnp.zeros_like(carry_ref)

    halo = halo_ref[...].astype(F32)
    halo = jnp.where(i == 0, jnp.zeros_like(halo), halo)
    xr = xr_ref[...].astype(F32)
    for n in range(N_RNN_BLOCKS):
        sl = slice(n * RNN_BLOCK, (n + 1) * RNN_BLOCK)
        ext_ref[n, 0:HALO, :] = halo[:, sl]
        ext_ref[n, HALO:, :] = xr[:, sl]

    c4 = (-0.5 * RG_C) * jnp.logaddexp(-lam_ref[...], 0.0)
    for n in range(N_RNN_BLOCKS):
        sl = slice(n * RNN_BLOCK, (n + 1) * RNN_BLOCK)
        xc = cb_ref[:, sl]
        for k in range(RNN_CONV):
            xc = xc + cw_ref[RNN_CONV - 1 - k:RNN_CONV - k, sl] * ext_ref[n, HALO - k:HALO - k + tt, :]
        g = jnp.dot(xc.astype(BF16), wcat_ref[n], preferred_element_type=F32)
        tr = jnp.tanh(g[:, :RNN_BLOCK] + ba_ref[:, sl])
        ti = jnp.tanh(g[:, RNN_BLOCK:] + bx_ref[:, sl])
        log_a = c4[:, sl] * (tr + 1.0)
        a = jnp.exp(log_a)
        y = jnp.maximum(1.0 - a * a, 0.0)
        mult = y * lax.rsqrt(jnp.maximum(y, F32_TINY))
        hx = 0.5 * xc
        a_ref[n] = a
        b_ref[n] = mult * (hx * ti + hx)

    grp = RNN_SCAN_GROUP
    blk = 8 * grp
    rows = lax.broadcasted_iota(jnp.int32, (8, RNN_BLOCK), 0)

    def scan_block(j, carries):
        base = pl.multiple_of(j * blk, blk)
        out = []
        for n in range(N_RNN_BLOCKS):
            cum_a, cum_b = [], []
            for r in range(grp):
                av = a_ref[n, pl.ds(base + r, 8, stride=grp), :]
                bv = b_ref[n, pl.ds(base + r, 8, stride=grp), :]
                if r > 0:
                    bv = av * cum_b[-1] + bv
                    av = av * cum_a[-1]
                cum_a.append(av)
                cum_b.append(bv)
            ga, gb = cum_a[-1], cum_b[-1]
            for d in (1, 2, 4):
                keep = rows >= d
                a_sh = jnp.where(keep, pltpu.roll(ga, d, axis=0), 1.0)
                b_sh = jnp.where(keep, pltpu.roll(gb, d, axis=0), 0.0)
                gb = ga * b_sh + gb
                ga = ga * a_sh
            h_end = ga * carries[n] + gb
            h_in = jnp.where(rows == 0, carries[n], pltpu.roll(h_end, 1, axis=0))
            for r in range(grp):
                h_ref[n, pl.ds(base + r, 8, stride=grp), :] = cum_a[r] * h_in + cum_b[r]
            out.append(jnp.broadcast_to(h_end[7:8, :], (8, RNN_BLOCK)))
        return tuple(out)

    carries = tuple(carry_ref[n] for n in range(N_RNN_BLOCKS))
    carries = lax.fori_loop(0, tt // blk, scan_block, carries, unroll=2)
    for n in range(N_RNN_BLOCKS):
        sl = slice(n * RNN_BLOCK, (n + 1) * RNN_BLOCK)
        carry_ref[n] = carries[n]
        o_ref[:, sl] = (h_ref[n] * jax.nn.gelu(gr_ref[:, sl].astype(F32))).astype(o_ref.dtype)


def _rglru(proj3, cw, cb, wcat, ba, bx, lam):
    b, s, _ = proj3.shape
    tt = RNN_TT
    hb = tt // HALO
    vec = pl.BlockSpec((1, D_RNN), lambda bi, i: (0, 0))
    return pl.pallas_call(
        _rglru_kernel,
        grid=(b, s // tt),
        in_specs=[
            pl.BlockSpec((None, tt, D_RNN), lambda bi, i: (bi, i, 0)),
            pl.BlockSpec((None, tt, D_RNN), lambda bi, i: (bi, i, 1)),
            pl.BlockSpec((None, HALO, D_RNN), lambda bi, i: (bi, jnp.maximum(i * hb - 1, 0), 0)),
            pl.BlockSpec((RNN_CONV, D_RNN), lambda bi, i: (0, 0)),
            vec,
            pl.BlockSpec((N_RNN_BLOCKS, RNN_BLOCK, 2 * RNN_BLOCK), lambda bi, i: (0, 0, 0)),
            vec, vec, vec,
        ],
        out_specs=pl.BlockSpec((None, tt, D_RNN), lambda bi, i: (bi, i, 0)),
        out_shape=jax.ShapeDtypeStruct((b, s, D_RNN), BF16),
        scratch_shapes=[pltpu.VMEM((N_RNN_BLOCKS, 8, RNN_BLOCK), F32),
                        pltpu.VMEM((N_RNN_BLOCKS, HALO + tt, RNN_BLOCK), F32),
                        pltpu.VMEM((N_RNN_BLOCKS, tt, RNN_BLOCK), F32),
                        pltpu.VMEM((N_RNN_BLOCKS, tt, RNN_BLOCK), F32),
                        pltpu.VMEM((N_RNN_BLOCKS, tt, RNN_BLOCK), F32)],
        compiler_params=pltpu.CompilerParams(
            dimension_semantics=("arbitrary", "arbitrary"),
            vmem_limit_bytes=VMEM_LIMIT_BYTES),
        name="rglru",
    )(proj3, proj3, proj3, cw, cb, wcat, ba, bx, lam)


def _attn_kernel(q_ref, qn_ref, k_ref, v_ref, g_ref, lq1_ref, lk1_ref, lq2_ref, lk2_ref,
                 o_ref, vt_ref, qq_ref, qqn_ref, m_ref, acc_ref, st_ref, mx_ref):
    qi = pl.program_id(2)
    t = ATTN_T
    cw = ATTN_CW
    nk = vt_ref.shape[1]
    nch = 2 * t // cw
    heads = range(ATTN_HEADS)

    def head_cols(hh):
        return slice(hh * V_HEAD_DIM, (hh + 1) * V_HEAD_DIM)

    def stack_queries(src_ref, dst_ref):
        for hh in heads:
            qs = (src_ref[:, head_cols(hh)].astype(F32) * (ATTN_SCALE * LOG2_E)).astype(BF16)
            lane = lax.broadcasted_iota(jnp.int32, qs.shape, 1)
            zero = jnp.zeros_like(qs)
            dst_ref[hh, 0:t, :] = jnp.where(lane < HEAD_DIM, qs, zero)
            dst_ref[hh, t:, :] = jnp.where(lane >= HEAD_DIM, qs, zero)

    def scores(ki, hh, c, slot, queries_ref):
        off = pl.multiple_of(ki * t, t)
        st = lax.dot_general(
            k_ref[pl.ds(off, t), head_cols(hh)], queries_ref[hh, c * cw:(c + 1) * cw, :],
            (((1,), (1,)), ((), ())), preferred_element_type=F32)
        st_ref[slot, hh, c] = st
        mx_ref[slot, hh, c] = jnp.max(st, axis=0, keepdims=True)

    stack_queries(q_ref, qq_ref)
    m_ref[...] = jnp.full_like(m_ref, -jnp.inf)
    acc_ref[...] = jnp.zeros_like(acc_ref)

    @pl.when(qi == 0)
    def _():
        ones = jnp.ones((ATTN_VT_ROWS - V_HEAD_DIM, t), BF16)
        for hh in heads:
            for j in range(nk):
                vt_ref[hh, j, 0:V_HEAD_DIM, :] = v_ref[j * t:(j + 1) * t, head_cols(hh)].T
                vt_ref[hh, j, V_HEAD_DIM:, :] = ones
        for c in range(nch):
            for hh in heads:
                scores(0, hh, c, 0, qq_ref)

    def softmax_pv(ki, hh, c, slot, masked):
        cs = slice(c * cw, (c + 1) * cw)
        q0 = (c * cw) % t
        rows = min(t, q0 + cw) if masked else t
        st = st_ref[slot, hh, c, 0:rows, :]
        if masked:
            r = lax.broadcasted_iota(jnp.int32, st.shape, 0)
            qidx = lax.broadcasted_iota(jnp.int32, st.shape, 1) + q0
            st = jnp.where(r <= qidx, st, -jnp.inf)
            m_loc = jnp.max(st, axis=0, keepdims=True)
        else:
            m_loc = mx_ref[slot, hh, c]
        m_old = m_ref[hh, :, cs]
        m_new = jnp.maximum(m_old, m_loc)
        alpha = jnp.exp2(m_old - m_new)
        p = jnp.exp2(st - m_new).astype(BF16)
        acc_ref[hh, :, cs] = alpha * acc_ref[hh, :, cs] + jnp.dot(
            vt_ref[hh, ki, :, 0:rows], p, preferred_element_type=F32)
        m_ref[hh, :, cs] = m_new

    def tile(ki, slot, masked, has_next):
        if has_next:
            for hh in heads:
                scores(ki + 1, hh, 0, 1 - slot, qq_ref)
        for c in range(nch):
            for hh in heads:
                if has_next and c + 1 < nch:
                    scores(ki + 1, hh, c + 1, 1 - slot, qq_ref)
                softmax_pv(ki, hh, c, slot, masked)

    def run_tiles(k0, n_full, then_diag):
        for i in range(n_full):
            tile(k0 + i, i % 2, False, True)
        if then_diag:
            tile(k0 + n_full, n_full % 2, True, False)

    def group_body(j, carry):
        run_tiles(j * ATTN_GROUP, ATTN_GROUP, False)
        return carry

    n_groups = qi // ATTN_GROUP
    lax.fori_loop(0, n_groups, group_body, 0)
    for rem in range(ATTN_GROUP):
        @pl.when(qi % ATTN_GROUP == rem)
        def _():
            run_tiles(n_groups * ATTN_GROUP, rem, True)

    stack_queries(qn_ref, qqn_ref)
    for c in range(nch):
        for hh in heads:
            scores(0, hh, c, 0, qqn_ref)

    lam = (jnp.exp(jnp.sum(lq1_ref[...] * lk1_ref[...], axis=-1, keepdims=True))
           - jnp.exp(jnp.sum(lq2_ref[...] * lk2_ref[...], axis=-1, keepdims=True))
           + LAMBDA_INIT)
    for hh in heads:
        denom = acc_ref[hh, V_HEAD_DIM:V_HEAD_DIM + 1, :]
        o = acc_ref[hh, 0:V_HEAD_DIM, :] * (1.0 / denom)
        head = o[:, :t] - lam * o[:, t:]
        ms = jnp.mean(head * head, axis=0, keepdims=True)
        y = head * lax.rsqrt(ms + SUBLN_EPS) * g_ref[...] * (1.0 - LAMBDA_INIT)
        o_ref[:, head_cols(hh)] = y.T.astype(o_ref.dtype)


def _attention(proj3, subln_g, lq1, lk1, lq2, lk2):
    b, s, _ = proj3.shape
    t = ATTN_T
    nq = s // t
    width = ATTN_HEADS * V_HEAD_DIM
    q_blk0 = 2 * D_RNN // width
    k_blk0 = q_blk0 + D_MODEL // width
    v_blk0 = k_blk0 + D_MODEL // width
    lvec = pl.BlockSpec((1, HEAD_DIM), lambda bi, h, qi: (0, 0))
    nch = 2 * t // ATTN_CW
    return pl.pallas_call(
        _attn_kernel,
        grid=(b, N_HEADS // ATTN_HEADS, nq),
        in_specs=[
            pl.BlockSpec((None, t, width), lambda bi, h, qi: (bi, qi, q_blk0 + h)),
            pl.BlockSpec((None, t, width),
                         lambda bi, h, qi: (bi, jnp.minimum(qi + 1, nq - 1), q_blk0 + h)),
            pl.BlockSpec((None, s, width), lambda bi, h, qi: (bi, 0, k_blk0 + h)),
            pl.BlockSpec((None, s, width), lambda bi, h, qi: (bi, 0, v_blk0 + h)),
            pl.BlockSpec((V_HEAD_DIM, 1), lambda bi, h, qi: (0, 0)),
            lvec, lvec, lvec, lvec,
        ],
        out_specs=pl.BlockSpec((None, t, width), lambda bi, h, qi: (bi, qi, h)),
        out_shape=jax.ShapeDtypeStruct((b, s, D_MODEL), BF16),
        scratch_shapes=[pltpu.VMEM((ATTN_HEADS, nq, ATTN_VT_ROWS, t), BF16),
                        pltpu.VMEM((ATTN_HEADS, 2 * t, V_HEAD_DIM), BF16),
                        pltpu.VMEM((ATTN_HEADS, 2 * t, V_HEAD_DIM), BF16),
                        pltpu.VMEM((ATTN_HEADS, 1, 2 * t), F32),
                        pltpu.VMEM((ATTN_HEADS, ATTN_VT_ROWS, 2 * t), F32),
                        pltpu.VMEM((2, ATTN_HEADS, nch, t, ATTN_CW), F32),
                        pltpu.VMEM((2, ATTN_HEADS, nch, 1, ATTN_CW), F32)],
        compiler_params=pltpu.CompilerParams(
            dimension_semantics=("arbitrary", "arbitrary", "arbitrary"),
            vmem_limit_bytes=VMEM_LIMIT_BYTES),
        name="diff_attn",
    )(proj3, proj3, proj3, proj3, subln_g, lq1, lk1, lq2, lk2)


def _merge_kernel(x_ref, yr_ref, ya_ref, gr_ref, ga_ref, pa_ref, pb_ref, wo_ref, o_ref):
    a = jnp.dot(yr_ref[...], pa_ref[...], preferred_element_type=F32)
    b = jnp.dot(ya_ref[...], pb_ref[...], preferred_element_type=F32)
    merged = (jax.nn.sigmoid(gr_ref[...].astype(F32)) * a
              + jax.nn.sigmoid(ga_ref[...].astype(F32)) * b)
    o_ref[...] = x_ref[...] + jnp.dot(merged.astype(BF16), wo_ref[...],
                                      preferred_element_type=F32)


def _merge(x2, y_rnn, y_attn, proj, pa, pb, wo):
    t = x2.shape[0]
    tm = MERGE_TM
    row = lambda cb: pl.BlockSpec((tm, D_MODEL), lambda i: (i, cb))
    wspec = pl.BlockSpec((D_MODEL, D_MODEL), lambda i: (0, 0))
    return pl.pallas_call(
        _merge_kernel,
        grid=(t // tm,),
        in_specs=[row(0), row(0), row(0), row(5), row(6), wspec, wspec, wspec],
        out_specs=row(0),
        out_shape=jax.ShapeDtypeStruct((t, D_MODEL), F32),
        compiler_params=pltpu.CompilerParams(
            dimension_semantics=("arbitrary",),
            vmem_limit_bytes=VMEM_LIMIT_BYTES),
        name="merge_out_proj",
    )(x2, y_rnn, y_attn, proj, proj, pa, pb, wo)


def _ffn_kernel(tiles_per_seq, x_ref, halo_ref, g_ref, wup_ref, cw_ref, cb_ref,
                wd_ref, fg_ref, o_ref, h_ref, u_ref, uv_ref, acc_ref):
    i = pl.program_id(0)
    tm = x_ref.shape[0]
    tf = FFN_TF
    nf = D_FF // tf
    lanes = u_ref.shape[-1]

    hh = _rms(halo_ref[...], g_ref[...], EPS)
    hh = jnp.where(i % tiles_per_seq == 0, jnp.zeros_like(hh), hh)
    h_ref[0:HALO, :] = hh.astype(BF16)
    h_ref[HALO:, :] = _rms(x_ref[...], g_ref[...], EPS).astype(BF16)

    def up(c, slot):
        u = jnp.dot(h_ref[...], wup_ref[:, c * tf:(c + 1) * tf],
                    preferred_element_type=F32)
        for s in range(tf // lanes):
            u_ref[slot, s] = u[:, s * lanes:(s + 1) * lanes]
        uv_ref[slot] = jnp.dot(h_ref[HALO:, :], wup_ref[:, D_FF + c * tf:D_FF + (c + 1) * tf],
                               preferred_element_type=F32)

    def act_down(c, slot):
        parts = []
        for s in range(tf // lanes):
            col = slice(c * tf + s * lanes, c * tf + (s + 1) * lanes)
            y = cb_ref[:, col]
            for k in range(FFN_CONV):
                y = y + (cw_ref[FFN_CONV - 1 - k:FFN_CONV - k, col]
                         * u_ref[slot, s, HALO - k:HALO - k + tm, :])
            parts.append((jax.nn.gelu(y) * uv_ref[slot, :, s * lanes:(s + 1) * lanes]
                          ).astype(BF16))
        act = jnp.concatenate(parts, axis=1)
        d = jnp.dot(act, wd_ref[c * tf:(c + 1) * tf, :], preferred_element_type=F32)
        if c == 0:
            acc_ref[...] = d
        else:
            acc_ref[...] += d

    nslot = u_ref.shape[0]
    for c in range(nslot - 1):
        up(c, c)
    for c in range(nf):
        if c + nslot - 1 < nf:
            up(c + nslot - 1, (c + nslot - 1) % nslot)
        act_down(c, c % nslot)

    o_ref[...] = _rms(x_ref[...] + acc_ref[...], fg_ref[...], EPS)


def _ffn(x1, g, wup, cw, cb, wd, fg, seq_len):
    t = x1.shape[0]
    tm, tf = FFN_TM, FFN_TF
    hb = tm // HALO
    lanes = 128
    const = lambda shape: pl.BlockSpec(shape, lambda i: (0, 0), pipeline_mode=pl.Buffered(1))
    return pl.pallas_call(
        functools.partial(_ffn_kernel, seq_len // tm),
        grid=(t // tm,),
        in_specs=[
            pl.BlockSpec((tm, D_MODEL), lambda i: (i, 0)),
            pl.BlockSpec((HALO, D_MODEL), lambda i: (jnp.maximum(i * hb - 1, 0), 0)),
            const((1, D_MODEL)),
            const((D_MODEL, 2 * D_FF)),
            const((FFN_CONV, D_FF)),
            const((1, D_FF)),
            const((D_FF, D_MODEL)),
            const((1, D_MODEL)),
        ],
        out_specs=pl.BlockSpec((tm, D_MODEL), lambda i: (i, 0)),
        out_shape=jax.ShapeDtypeStruct((t, D_MODEL), F32),
        scratch_shapes=[pltpu.VMEM((HALO + tm, D_MODEL), BF16),
                        pltpu.VMEM((FFN_SLOTS, tf // lanes, HALO + tm, lanes), F32),
                        pltpu.VMEM((FFN_SLOTS, tm, tf), F32),
                        pltpu.VMEM((tm, D_MODEL), F32)],
        compiler_params=pltpu.CompilerParams(
            dimension_semantics=("arbitrary",),
            vmem_limit_bytes=VMEM_LIMIT_BYTES),
        name="conv_ffn",
    )(x1, x1, g, wup, cw, cb, wd, fg)


def kernel(x, attn_norm_g, w_in, rnn_conv_w, rnn_conv_b, rg_wa, rg_ba, rg_wx, rg_bx, rg_lambda, lam_q1, lam_k1, lam_q2, lam_k2, subln_g, w_proj_rnn, w_proj_attn, w_out, mlp_norm_g, w_up, ffn_conv_w, ffn_conv_b, w_down, final_norm_g):
    bsz, s, d = x.shape
    t = bsz * s
    x2 = x.reshape(t, d)
    l = 0
    row = lambda v: v.reshape(1, -1)

    proj = _inproj(x2, row(attn_norm_g[l]), w_in[l].astype(BF16))
    proj3 = proj.reshape(bsz, s, D_IN)

    wcat = (0.5 * jnp.concatenate([rg_wa[l], rg_wx[l]], axis=-1)).astype(BF16)
    y_rnn = _rglru(proj3, rnn_conv_w[l], row(rnn_conv_b[l]), wcat, row(0.5 * rg_ba[l]),
                   row(0.5 * rg_bx[l]), row(rg_lambda[l]))
    y_attn = _attention(proj3, subln_g[l].reshape(V_HEAD_DIM, 1), row(lam_q1[l]),
                        row(lam_k1[l]), row(lam_q2[l]), row(lam_k2[l]))

    x1 = _merge(x2, y_rnn.reshape(t, d), y_attn.reshape(t, d), proj,
                w_proj_rnn[l].astype(BF16), w_proj_attn[l].astype(BF16),
                w_out[l].astype(BF16))
    out = _ffn(x1, row(mlp_norm_g[l]), w_up[l].astype(BF16), ffn_conv_w[l],
               row(ffn_conv_b[l]), w_down[l].astype(BF16), row(final_norm_g), s)
    return out.reshape(bsz, s, d)
```

```python
import functools
import math

import jax
import jax.numpy as jnp
from jax import lax
from jax.experimental import pallas as pl
from jax.experimental.pallas import tpu as pltpu

F32 = jnp.float32
BF16 = jnp.bfloat16

D_MODEL = 1024
EPS = 1e-6
F32_TINY = float(jnp.finfo(jnp.float32).tiny)
SUBLN_EPS = 1e-5
D_RNN = D_MODEL
N_RNN_BLOCKS = 8
RNN_BLOCK = D_RNN // N_RNN_BLOCKS
RNN_CONV = 4
RG_C = 8.0
HEAD_DIM = 64
N_HEADS = D_MODEL // (2 * HEAD_DIM)
V_HEAD_DIM = 2 * HEAD_DIM
ATTN_SCALE = HEAD_DIM ** -0.5
D_FF = 3 * D_MODEL
FFN_CONV = 3
D_IN = 7 * D_MODEL
LAMBDA_INIT = 0.8 - 0.6 * math.exp(-0.3 * 0)

VMEM_LIMIT_BYTES = 56 * 1024 * 1024
HALO = 16

INPROJ_TM = 2048
INPROJ_TN = 1024
RNN_TT = 512
RNN_SCAN_GROUP = 8
ATTN_T = 512
ATTN_CW = 256
ATTN_VT_ROWS = V_HEAD_DIM + 16
ATTN_GROUP = 6
ATTN_HEADS = 2
LOG2_E = math.log2(math.e)
MERGE_TM = 512
FFN_TM = 512
FFN_TF = 512
FFN_SLOTS = 3


def _rms(x, g, eps):
    return x * lax.rsqrt(jnp.mean(x * x, axis=-1, keepdims=True) + eps) * g


def _inproj_kernel(x_ref, g_ref, w_ref, o_ref, h_ref):
    @pl.when(pl.program_id(1) == 0)
    def _():
        h_ref[...] = _rms(x_ref[...], g_ref[...], EPS).astype(BF16)

    o_ref[...] = jnp.dot(h_ref[...], w_ref[...],
                         preferred_element_type=F32).astype(o_ref.dtype)


def _inproj(x2, g, w):
    t = x2.shape[0]
    tm, tn = INPROJ_TM, INPROJ_TN
    return pl.pallas_call(
        _inproj_kernel,
        grid=(t // tm, D_IN // tn),
        in_specs=[
            pl.BlockSpec((tm, D_MODEL), lambda i, j: (i, 0)),
            pl.BlockSpec((1, D_MODEL), lambda i, j: (0, 0)),
            pl.BlockSpec((D_MODEL, tn), lambda i, j: (0, j)),
        ],
        out_specs=pl.BlockSpec((tm, tn), lambda i, j: (i, j)),
        out_shape=jax.ShapeDtypeStruct((t, D_IN), BF16),
        scratch_shapes=[pltpu.VMEM((tm, D_MODEL), BF16)],
        compiler_params=pltpu.CompilerParams(
            dimension_semantics=("arbitrary", "arbitrary"),
            vmem_limit_bytes=VMEM_LIMIT_BYTES),
        name="in_proj",
    )(x2, g, w)


def _rglru_kernel(xr_ref, gr_ref, halo_ref, cw_ref, cb_ref, wcat_ref, ba_ref, bx_ref,
                  lam_ref, o_ref, carry_ref, ext_ref, a_ref, b_ref, h_ref):
    i = pl.program_id(1)
    tt = xr_ref.shape[0]
    c = D_RNN

    @pl.when(i == 0)
    def _():
        carry_ref[...] = jnp.zeros_like(carry_ref)

    halo = halo_ref[...].astype(F32)
    halo = jnp.where(i == 0, jnp.zeros_like(halo), halo)
    xr = xr_ref[...].astype(F32)
    for n in range(N_RNN_BLOCKS):
        sl = slice(n * RNN_BLOCK, (n + 1) * RNN_BLOCK)
        ext_ref[n, 0:HALO, :] = halo[:, sl]
        ext_ref[n, HALO:, :] = xr[:, sl]

    c4 = (-0.5 * RG_C) * jnp.logaddexp(-lam_ref[...], 0.0)
    for n in range(N_RNN_BLOCKS):
        sl = slice(n * RNN_BLOCK, (n + 1) * RNN_BLOCK)
        xc = cb_ref[:, sl]
        for k in range(RNN_CONV):
            xc = xc + cw_ref[RNN_CONV - 1 - k:RNN_CONV - k, sl] * ext_ref[n, HALO - k:HALO - k + tt, :]
        g = jnp.dot(xc.astype(BF16), wcat_ref[n], preferred_element_type=F32)
        tr = jnp.tanh(g[:, :RNN_BLOCK] + ba_ref[:, sl])
        ti = jnp.tanh(g[:, RNN_BLOCK:] + bx_ref[:, sl])
        log_a = c4[:, sl] * (tr + 1.0)
        a = jnp.exp(log_a)
        y = jnp.maximum(1.0 - a * a, 0.0)
        mult = y * lax.rsqrt(jnp.maximum(y, F32_TINY))
        hx = 0.5 * xc
        a_ref[n] = a
        b_ref[n] = mult * (hx * ti + hx)

    grp = RNN_SCAN_GROUP
    blk = 8 * grp
    rows = lax.broadcasted_iota(jnp.int32, (8, RNN_BLOCK), 0)

    def scan_block(j, carries):
        base = pl.multiple_of(j * blk, blk)
        out = []
        for n in range(N_RNN_BLOCKS):
            cum_a, cum_b = [], []
            for r in range(grp):
                av = a_ref[n, pl.ds(base + r, 8, stride=grp), :]
                bv = b_ref[n, pl.ds(base + r, 8, stride=grp), :]
                if r > 0:
                    bv = av * cum_b[-1] + bv
                    av = av * cum_a[-1]
                cum_a.append(av)
                cum_b.append(bv)
            ga, gb = cum_a[-1], cum_b[-1]
            for d in (1, 2, 4):
                keep = rows >= d
                a_sh = jnp.where(keep, pltpu.roll(ga, d, axis=0), 1.0)
                b_sh = jnp.where(keep, pltpu.roll(gb, d, axis=0), 0.0)
                gb = ga * b_sh + gb
                ga = ga * a_sh
            h_end = ga * carries[n] + gb
            h_in = jnp.where(rows == 0, carries[n], pltpu.roll(h_end, 1, axis=0))
            for r in range(grp):
                h_ref[n, pl.ds(base + r, 8, stride=grp), :] = cum_a[r] * h_in + cum_b[r]
            out.append(jnp.broadcast_to(h_end[7:8, :], (8, RNN_BLOCK)))
        return tuple(out)

    carries = tuple(carry_ref[n] for n in range(N_RNN_BLOCKS))
    carries = lax.fori_loop(0, tt // blk, scan_block, carries, unroll=2)
    for n in range(N_RNN_BLOCKS):
        sl = slice(n * RNN_BLOCK, (n + 1) * RNN_BLOCK)
        carry_ref[n] = carries[n]
        o_ref[:, sl] = (h_ref[n] * jax.nn.gelu(gr_ref[:, sl].astype(F32))).astype(o_ref.dtype)


def _rglru(proj3, cw, cb, wcat, ba, bx, lam):
    b, s, _ = proj3.shape
    tt = RNN_TT
    hb = tt // HALO
    vec = pl.BlockSpec((1, D_RNN), lambda bi, i: (0, 0))
    return pl.pallas_call(
        _rglru_kernel,
        grid=(b, s // tt),
        in_specs=[
            pl.BlockSpec((None, tt, D_RNN), lambda bi, i: (bi, i, 0)),
            pl.BlockSpec((None, tt, D_RNN), lambda bi, i: (bi, i, 1)),
            pl.BlockSpec((None, HALO, D_RNN), lambda bi, i: (bi, jnp.maximum(i * hb - 1, 0), 0)),
            pl.BlockSpec((RNN_CONV, D_RNN), lambda bi, i: (0, 0)),
            vec,
            pl.BlockSpec((N_RNN_BLOCKS, RNN_BLOCK, 2 * RNN_BLOCK), lambda bi, i: (0, 0, 0)),
            vec, vec, vec,
        ],
        out_specs=pl.BlockSpec((None, tt, D_RNN), lambda bi, i: (bi, i, 0)),
        out_shape=jax.ShapeDtypeStruct((b, s, D_RNN), BF16),
        scratch_shapes=[pltpu.VMEM((N_RNN_BLOCKS, 8, RNN_BLOCK), F32),
                        pltpu.VMEM((N_RNN_BLOCKS, HALO + tt, RNN_BLOCK), F32),
                        pltpu.VMEM((N_RNN_BLOCKS, tt, RNN_BLOCK), F32),
                        pltpu.VMEM((N_RNN_BLOCKS, tt, RNN_BLOCK), F32),
                        pltpu.VMEM((N_RNN_BLOCKS, tt, RNN_BLOCK), F32)],
        compiler_params=pltpu.CompilerParams(
            dimension_semantics=("arbitrary", "arbitrary"),
            vmem_limit_bytes=VMEM_LIMIT_BYTES),
        name="rglru",
    )(proj3, proj3, proj3, cw, cb, wcat, ba, bx, lam)


def _attn_kernel(q_ref, qn_ref, k_ref, v_ref, g_ref, lq1_ref, lk1_ref, lq2_ref, lk2_ref,
                 o_ref, vt_ref, qq_ref, qqn_ref, m_ref, acc_ref, st_ref, mx_ref):
    qi = pl.program_id(2)
    t = ATTN_T
    cw = ATTN_CW
    nk = vt_ref.shape[1]
    nch = 2 * t // cw
    heads = range(ATTN_HEADS)

    def head_cols(hh):
        return slice(hh * V_HEAD_DIM, (hh + 1) * V_HEAD_DIM)

    def stack_queries(src_ref, dst_ref):
        for hh in heads:
            qs = (src_ref[:, head_cols(hh)].astype(F32) * (ATTN_SCALE * LOG2_E)).astype(BF16)
            lane = lax.broadcasted_iota(jnp.int32, qs.shape, 1)
            zero = jnp.zeros_like(qs)
            dst_ref[hh, 0:t, :] = jnp.where(lane < HEAD_DIM, qs, zero)
            dst_ref[hh, t:, :] = jnp.where(lane >= HEAD_DIM, qs, zero)

    def scores(ki, hh, c, slot, queries_ref):
        off = pl.multiple_of(ki * t, t)
        st = lax.dot_general(
            k_ref[pl.ds(off, t), head_cols(hh)], queries_ref[hh, c * cw:(c + 1) * cw, :],
            (((1,), (1,)), ((), ())), preferred_element_type=F32)
        st_ref[slot, hh, c] = st
        mx_ref[slot, hh, c] = jnp.max(st, axis=0, keepdims=True)

    stack_queries(q_ref, qq_ref)
    m_ref[...] = jnp.full_like(m_ref, -jnp.inf)
    acc_ref[...] = jnp.zeros_like(acc_ref)

    @pl.when(qi == 0)
    def _():
        ones = jnp.ones((ATTN_VT_ROWS - V_HEAD_DIM, t), BF16)
        for hh in heads:
            for j in range(nk):
                vt_ref[hh, j, 0:V_HEAD_DIM, :] = v_ref[j * t:(j + 1) * t, head_cols(hh)].T
                vt_ref[hh, j, V_HEAD_DIM:, :] = ones
        for c in range(nch):
            for hh in heads:
                scores(0, hh, c, 0, qq_ref)

    def softmax_pv(ki, hh, c, slot, masked):
        cs = slice(c * cw, (c + 1) * cw)
        q0 = (c * cw) % t
        rows = min(t, q0 + cw) if masked else t
        st = st_ref[slot, hh, c, 0:rows, :]
        if masked:
            r = lax.broadcasted_iota(jnp.int32, st.shape, 0)
            qidx = lax.broadcasted_iota(jnp.int32, st.shape, 1) + q0
            st = jnp.where(r <= qidx, st, -jnp.inf)
            m_loc = jnp.max(st, axis=0, keepdims=True)
        else:
            m_loc = mx_ref[slot, hh, c]
        m_old = m_ref[hh, :, cs]
        m_new = jnp.maximum(m_old, m_loc)
        alpha = jnp.exp2(m_old - m_new)
        p = jnp.exp2(st - m_new).astype(BF16)
        acc_ref[hh, :, cs] = alpha * acc_ref[hh, :, cs] + jnp.dot(
            vt_ref[hh, ki, :, 0:rows], p, preferred_element_type=F32)
        m_ref[hh, :, cs] = m_new

    def tile(ki, slot, masked, has_next):
        if has_next:
            for hh in heads:
                scores(ki + 1, hh, 0, 1 - slot, qq_ref)
        for c in range(nch):
            for hh in heads:
                if has_next and c + 1 < nch:
                    scores(ki + 1, hh, c + 1, 1 - slot, qq_ref)
                softmax_pv(ki, hh, c, slot, masked)

    def run_tiles(k0, n_full, then_diag):
        for i in range(n_full):
            tile(k0 + i, i % 2, False, True)
        if then_diag:
            tile(k0 + n_full, n_full % 2, True, False)

    def group_body(j, carry):
        run_tiles(j * ATTN_GROUP, ATTN_GROUP, False)
        return carry

    n_groups = qi // ATTN_GROUP
    lax.fori_loop(0, n_groups, group_body, 0)
    for rem in range(ATTN_GROUP):
        @pl.when(qi % ATTN_GROUP == rem)
        def _():
            run_tiles(n_groups * ATTN_GROUP, rem, True)

    stack_queries(qn_ref, qqn_ref)
    for c in range(nch):
        for hh in heads:
            scores(0, hh, c, 0, qqn_ref)

    lam = (jnp.exp(jnp.sum(lq1_ref[...] * lk1_ref[...], axis=-1, keepdims=True))
           - jnp.exp(jnp.sum(lq2_ref[...] * lk2_ref[...], axis=-1, keepdims=True))
           + LAMBDA_INIT)
    for hh in heads:
        denom = acc_ref[hh, V_HEAD_DIM:V_HEAD_DIM + 1, :]
        o = acc_ref[hh, 0:V_HEAD_DIM, :] * (1.0 / denom)
        head = o[:, :t] - lam * o[:, t:]
        ms = jnp.mean(head * head, axis=0, keepdims=True)
        y = head * lax.rsqrt(ms + SUBLN_EPS) * g_ref[...] * (1.0 - LAMBDA_INIT)
        o_ref[:, head_cols(hh)] = y.T.astype(o_ref.dtype)


def _attention(proj3, subln_g, lq1, lk1, lq2, lk2):
    b, s, _ = proj3.shape
    t = ATTN_T
    nq = s // t
    width = ATTN_HEADS * V_HEAD_DIM
    q_blk0 = 2 * D_RNN // width
    k_blk0 = q_blk0 + D_MODEL // width
    v_blk0 = k_blk0 + D_MODEL // width
    lvec = pl.BlockSpec((1, HEAD_DIM), lambda bi, h, qi: (0, 0))
    nch = 2 * t // ATTN_CW
    return pl.pallas_call(
        _attn_kernel,
        grid=(b, N_HEADS // ATTN_HEADS, nq),
        in_specs=[
            pl.BlockSpec((None, t, width), lambda bi, h, qi: (bi, qi, q_blk0 + h)),
            pl.BlockSpec((None, t, width),
                         lambda bi, h, qi: (bi, jnp.minimum(qi + 1, nq - 1), q_blk0 + h)),
            pl.BlockSpec((None, s, width), lambda bi, h, qi: (bi, 0, k_blk0 + h)),
            pl.BlockSpec((None, s, width), lambda bi, h, qi: (bi, 0, v_blk0 + h)),
            pl.BlockSpec((V_HEAD_DIM, 1), lambda bi, h, qi: (0, 0)),
            lvec, lvec, lvec, lvec,
        ],
        out_specs=pl.BlockSpec((None, t, width), lambda bi, h, qi: (bi, qi, h)),
        out_shape=jax.ShapeDtypeStruct((b, s, D_MODEL), BF16),
        scratch_shapes=[pltpu.VMEM((ATTN_HEADS, nq, ATTN_VT_ROWS, t), BF16),
                        pltpu.VMEM((ATTN_HEADS, 2 * t, V_HEAD_DIM), BF16),
                        pltpu.VMEM((ATTN_HEADS, 2 * t, V_HEAD_DIM), BF16),
                        pltpu.VMEM((ATTN_HEADS, 1, 2 * t), F32),
                        pltpu.VMEM((ATTN_HEADS, ATTN_VT_ROWS, 2 * t), F32),
                        pltpu.VMEM((2, ATTN_HEADS, nch, t, ATTN_CW), F32),
                        pltpu.VMEM((2, ATTN_HEADS, nch, 1, ATTN_CW), F32)],
        compiler_params=pltpu.CompilerParams(
            dimension_semantics=("arbitrary", "arbitrary", "arbitrary"),
            vmem_limit_bytes=VMEM_LIMIT_BYTES),
        name="diff_attn",
    )(proj3, proj3, proj3, proj3, subln_g, lq1, lk1, lq2, lk2)


def _merge_kernel(x_ref, yr_ref, ya_ref, gr_ref, ga_ref, pa_ref, pb_ref, wo_ref, o_ref):
    a = jnp.dot(yr_ref[...], pa_ref[...], preferred_element_type=F32)
    b = jnp.dot(ya_ref[...], pb_ref[...], preferred_element_type=F32)
    merged = (jax.nn.sigmoid(gr_ref[...].astype(F32)) * a
              + jax.nn.sigmoid(ga_ref[...].astype(F32)) * b)
    o_ref[...] = x_ref[...] + jnp.dot(merged.astype(BF16), wo_ref[...],
                                      preferred_element_type=F32)


def _merge(x2, y_rnn, y_attn, proj, pa, pb, wo):
    t = x2.shape[0]
    tm = MERGE_TM
    row = lambda cb: pl.BlockSpec((tm, D_MODEL), lambda i: (i, cb))
    wspec = pl.BlockSpec((D_MODEL, D_MODEL), lambda i: (0, 0))
    return pl.pallas_call(
        _merge_kernel,
        grid=(t // tm,),
        in_specs=[row(0), row(0), row(0), row(5), row(6), wspec, wspec, wspec],
        out_specs=row(0),
        out_shape=jax.ShapeDtypeStruct((t, D_MODEL), F32),
        compiler_params=pltpu.CompilerParams(
            dimension_semantics=("arbitrary",),
            vmem_limit_bytes=VMEM_LIMIT_BYTES),
        name="merge_out_proj",
    )(x2, y_rnn, y_attn, proj, proj, pa, pb, wo)


def _ffn_kernel(tiles_per_seq, x_ref, halo_ref, g_ref, wup_ref, cw_ref, cb_ref,
                wd_ref, fg_ref, o_ref, h_ref, u_ref, uv_ref, acc_ref):
    i = pl.program_id(0)
    tm = x_ref.shape[0]
    tf = FFN_TF
    nf = D_FF // tf
    lanes = u_ref.shape[-1]

    hh = _rms(halo_ref[...], g_ref[...], EPS)
    hh = jnp.where(i % tiles_per_seq == 0, jnp.zeros_like(hh), hh)
    h_ref[0:HALO, :] = hh.astype(BF16)
    h_ref[HALO:, :] = _rms(x_ref[...], g_ref[...], EPS).astype(BF16)

    def up(c, slot):
        u = jnp.dot(h_ref[...], wup_ref[:, c * tf:(c + 1) * tf],
                    preferred_element_type=F32)
        for s in range(tf // lanes):
            u_ref[slot, s] = u[:, s * lanes:(s + 1) * lanes]
        uv_ref[slot] = jnp.dot(h_ref[HALO:, :], wup_ref[:, D_FF + c * tf:D_FF + (c + 1) * tf],
                               preferred_element_type=F32)

    def act_down(c, slot):
        parts = []
        for s in range(tf // lanes):
            col = slice(c * tf + s * lanes, c * tf + (s + 1) * lanes)
            y = cb_ref[:, col]
            for k in range(FFN_CONV):
                y = y + (cw_ref[FFN_CONV - 1 - k:FFN_CONV - k, col]
                         * u_ref[slot, s, HALO - k:HALO - k + tm, :])
            parts.append((jax.nn.gelu(y) * uv_ref[slot, :, s * lanes:(s + 1) * lanes]
                          ).astype(BF16))
        act = jnp.concatenate(parts, axis=1)
        d = jnp.dot(act, wd_ref[c * tf:(c + 1) * tf, :], preferred_element_type=F32)
        if c == 0:
            acc_ref[...] = d
        else:
            acc_ref[...] += d

    nslot = u_ref.shape[0]
    for c in range(nslot - 1):
        up(c, c)
    for c in range(nf):
        if c + nslot - 1 < nf:
            up(c + nslot - 1, (c + nslot - 1) % nslot)
        act_down(c, c % nslot)

    o_ref[...] = _rms(x_ref[...] + acc_ref[...], fg_ref[...], EPS)


def _ffn(x1, g, wup, cw, cb, wd, fg, seq_len):
    t = x1.shape[0]
    tm, tf = FFN_TM, FFN_TF
    hb = tm // HALO
    lanes = 128
    const = lambda shape: pl.BlockSpec(shape, lambda i: (0, 0), pipeline_mode=pl.Buffered(1))
    return pl.pallas_call(
        functools.partial(_ffn_kernel, seq_len // tm),
        grid=(t // tm,),
        in_specs=[
            pl.BlockSpec((tm, D_MODEL), lambda i: (i, 0)),
            pl.BlockSpec((HALO, D_MODEL), lambda i: (jnp.maximum(i * hb - 1, 0), 0)),
            const((1, D_MODEL)),
            const((D_MODEL, 2 * D_FF)),
            const((FFN_CONV, D_FF)),
            const((1, D_FF)),
            const((D_FF, D_MODEL)),
            const((1, D_MODEL)),
        ],
        out_specs=pl.BlockSpec((tm, D_MODEL), lambda i: (i, 0)),
        out_shape=jax.ShapeDtypeStruct((t, D_MODEL), F32),
        scratch_shapes=[pltpu.VMEM((HALO + tm, D_MODEL), BF16),
                        pltpu.VMEM((FFN_SLOTS, tf // lanes, HALO + tm, lanes), F32),
                        pltpu.VMEM((FFN_SLOTS, tm, tf), F32),
                        pltpu.VMEM((tm, D_MODEL), F32)],
        compiler_params=pltpu.CompilerParams(
            dimension_semantics=("arbitrary",),
            vmem_limit_bytes=VMEM_LIMIT_BYTES),
        name="conv_ffn",
    )(x1, x1, g, wup, cw, cb, wd, fg)


def kernel(x, attn_norm_g, w_in, rnn_conv_w, rnn_conv_b, rg_wa, rg_ba, rg_wx, rg_bx, rg_lambda, lam_q1, lam_k1, lam_q2, lam_k2, subln_g, w_proj_rnn, w_proj_attn, w_out, mlp_norm_g, w_up, ffn_conv_w, ffn_conv_b, w_down, final_norm_g):
    bsz, s, d = x.shape
    t = bsz * s
    x2 = x.reshape(t, d)
    l = 0
    row = lambda v: v.reshape(1, -1)

    proj = _inproj(x2, row(attn_norm_g[l]), w_in[l].astype(BF16))
    proj3 = proj.reshape(bsz, s, D_IN)

    wcat = (0.5 * jnp.concatenate([rg_wa[l], rg_wx[l]], axis=-1)).astype(BF16)
    y_rnn = _rglru(proj3, rnn_conv_w[l], row(rnn_conv_b[l]), wcat, row(0.5 * rg_ba[l]),
                   row(0.5 * rg_bx[l]), row(rg_lambda[l]))
    y_attn = _attention(proj3, subln_g[l].reshape(V_HEAD_DIM, 1), row(lam_q1[l]),
                        row(lam_k1[l]), row(lam_q2[l]), row(lam_k2[l]))

    x1 = _merge(x2, y_rnn.reshape(t, d), y_attn.reshape(t, d), proj,
                w_proj_rnn[l].astype(BF16), w_proj_attn[l].astype(BF16),
                w_out[l].astype(BF16))
    out = _ffn(x1, row(mlp_norm_g[l]), w_up[l].astype(BF16), ffn_conv_w[l],
               row(ffn_conv_b[l]), w_down[l].astype(BF16), row(final_norm_g), s)
    return out.reshape(bsz, s, d)
```
